```python
import jax, jax.numpy as jnp
from jax import lax
import numpy as np

D_MODEL = 1024
BATCH = 4
SEQ = 4096
DEPTH = 4
DEC_BATCH = 128
DEC_SEQ = 8
PAST_LEN = 8192
PAGE_SIZE = 128

N_MIXERS = 2
CONV_WIDTH = 3
MLA_HEADS = D_MODEL // 128
Q_RANK = 3 * D_MODEL // 8
KV_RANK = D_MODEL // 4
NOPE_DIM = 128
ROPE_DIM = 64
V_DIM = 128
ROPE_THETA = 10000.0
Q_BLOCK = 128
SCORE_SCALE = (NOPE_DIM + ROPE_DIM) ** -0.5
N_MEM = 256
XA_HEADS = 4
XA_HEAD_DIM = D_MODEL // XA_HEADS
XA_SCALE = XA_HEAD_DIM ** -0.5
D_FF = ((8 * D_MODEL // 3 + 127) // 128) * 128
RMS_EPS = 1e-6

kernel_name = 'hybrid_shortconv_mla_memxattn_convffn_step'


def rmsnorm(x, g):
    xf = x.astype(jnp.float32)
    y = xf * lax.rsqrt(jnp.mean(xf * xf, axis=-1, keepdims=True) + RMS_EPS)
    return (y * g.astype(jnp.float32)).astype(x.dtype)


def rope(x, pos):
    half = ROPE_DIM // 2
    inv = 1.0 / (ROPE_THETA ** (jnp.arange(half, dtype=jnp.float32) * (2.0 / ROPE_DIM)))
    ang = pos.astype(jnp.float32)[:, None] * inv[None, :]
    cos = jnp.cos(ang)[None, :, None, :]
    sin = jnp.sin(ang)[None, :, None, :]
    xf = x.astype(jnp.float32)
    x1, x2 = xf[..., :half], xf[..., half:]
    return jnp.concatenate([x1 * cos - x2 * sin, x2 * cos + x1 * sin], axis=-1).astype(x.dtype)


def causal_dwconv3(u, prefix, w):
    t = u.shape[1]
    ext = jnp.concatenate([prefix.astype(u.dtype), u], axis=1)
    y = ext[:, :t] * w[0] + ext[:, 1:t + 1] * w[1] + ext[:, 2:] * w[2]
    return y, ext[:, -(CONV_WIDTH - 1):]


def short_conv_mixer(h, prefix, w_in, conv_w, w_out):
    z = jnp.einsum('btd,de->bte', h, w_in)
    b, c, hh = jnp.split(z, 3, axis=-1)
    u, new_prefix = causal_dwconv3(c * hh, prefix, conv_w)
    return jnp.einsum('btd,de->bte', b * u, w_out), new_prefix


def mla_project(h, pos, w_down, q_norm_g, kv_norm_g, w_uq):
    d = jnp.einsum('btd,dc->btc', h, w_down)
    c_q = rmsnorm(d[..., :Q_RANK], q_norm_g)
    c_kv = rmsnorm(d[..., Q_RANK:Q_RANK + KV_RANK], kv_norm_g)
    k_r = rope(d[..., Q_RANK + KV_RANK:][:, :, None, :], pos)[:, :, 0, :]
    q = jnp.einsum('btc,chd->bthd', c_q, w_uq)
    return q[..., :NOPE_DIM], rope(q[..., NOPE_DIM:], pos), c_kv, k_r


def mla_prompt_attention(q_nope, q_rope, k_nope, k_rope, v):
    bsz, s_len = q_nope.shape[0], q_nope.shape[1]
    nb = s_len // Q_BLOCK

    def blocks(a):
        return a.reshape((bsz, nb, Q_BLOCK) + a.shape[2:]).swapaxes(0, 1)

    kpos = jnp.arange(s_len)
    qpos = kpos.reshape(nb, Q_BLOCK)

    def one_block(args):
        qn, qr, qp = args
        s = (jnp.einsum('bqhd,bkhd->bhqk', qn, k_nope)
             + jnp.einsum('bqhd,bkd->bhqk', qr, k_rope)).astype(jnp.float32) * SCORE_SCALE
        s = jnp.where(kpos[None, None, None, :] <= qp[None, None, :, None], s, -jnp.inf)
        p = jax.nn.softmax(s, axis=-1).astype(v.dtype)
        return jnp.einsum('bhqk,bkhd->bqhd', p, v)

    o = lax.map(one_block, (blocks(q_nope), blocks(q_rope), qpos))
    return o.swapaxes(0, 1).reshape(bsz, s_len, MLA_HEADS, V_DIM)


def mla_sample_attention(q_nope, q_rope, c_kv_new, k_r_new, lat_past, rope_past, w_uk, w_uv):
    q_lat = jnp.einsum('bthd,chd->bthc', q_nope, w_uk)
    s_past = (jnp.einsum('bthc,bkc->bhtk', q_lat, lat_past)
              + jnp.einsum('bthr,bkr->bhtk', q_rope, rope_past)).astype(jnp.float32) * SCORE_SCALE
    s_new = (jnp.einsum('bthc,bkc->bhtk', q_lat, c_kv_new)
             + jnp.einsum('bthr,bkr->bhtk', q_rope, k_r_new)).astype(jnp.float32) * SCORE_SCALE
    t = q_nope.shape[1]
    causal = jnp.arange(t)[None, :] <= jnp.arange(t)[:, None]
    s_new = jnp.where(causal[None, None], s_new, -jnp.inf)
    p = jax.nn.softmax(jnp.concatenate([s_past, s_new], axis=-1), axis=-1).astype(lat_past.dtype)
    n_past = lat_past.shape[1]
    o_lat = (jnp.einsum('bhtk,bkc->bthc', p[..., :n_past], lat_past)
             + jnp.einsum('bhtk,bkc->bthc', p[..., n_past:], c_kv_new))
    return jnp.einsum('bthc,chd->bthd', o_lat, w_uv)


def mem_keys_values(mem, mem_norm_g, w_kv):
    m = rmsnorm(mem, mem_norm_g)
    kv = jnp.einsum('bmd,dzhe->zbmhe', m, w_kv)
    return kv[0], kv[1]


def mem_cross_attention(h, mk, mv, w_q, w_o):
    q = jnp.einsum('btd,dhe->bthe', h, w_q)
    s = jnp.einsum('bthe,bmhe->bhtm', q, mk).astype(jnp.float32) * XA_SCALE
    p = jax.nn.softmax(s, axis=-1).astype(mv.dtype)
    o = jnp.einsum('bhtm,bmhe->bthe', p, mv)
    return jnp.einsum('bthe,hed->btd', o, w_o)


def conv_ffn(h, prefix, w_up, conv_w, w_down):
    u = jnp.einsum('btd,df->btf', h, w_up)
    g, val = jnp.split(u, [D_FF], axis=-1)
    gc, new_prefix = causal_dwconv3(g, prefix, conv_w)
    return jnp.einsum('btf,fd->btd', jax.nn.silu(gc) * val, w_down), new_prefix


def setup_inputs(seed: int = 0) -> dict:
    key = jax.random.key(seed)
    ks = iter(jax.random.split(key, 48))
    f32 = jnp.float32
    nc = (DEPTH + 1) // 2
    nm = DEPTH // 2

    def nrm(shape, scale=1.0):
        return jax.random.normal(next(ks), shape, f32) * scale

    def gain(shape):
        return 1.0 + 0.1 * nrm(shape)

    n_pages = PAST_LEN // PAGE_SIZE
    n_used = DEC_BATCH * n_pages
    n_pool = n_used + (n_used + 3) // 4
    perm = jax.random.permutation(next(ks), n_pool)
    page_table = perm[:n_used].reshape(DEC_BATCH, n_pages).astype(jnp.int32)

    return {
        'x_prompt': nrm((BATCH, SEQ, D_MODEL)),
        'x_sample': nrm((DEC_BATCH, DEC_SEQ, D_MODEL)),
        'state_conv': nrm((nc, DEC_BATCH, CONV_WIDTH - 1, D_MODEL)),
        'cache_kv_latent': nrm((nm, n_pool, PAGE_SIZE, KV_RANK)),
        'cache_k_rope': nrm((nm, n_pool, PAGE_SIZE, ROPE_DIM)),
        'state_ffn_conv': nrm((DEPTH, DEC_BATCH, CONV_WIDTH - 1, D_FF)),
        'cache_mem_k': nrm((DEPTH, DEC_BATCH, N_MEM, XA_HEADS, XA_HEAD_DIM)),
        'cache_mem_v': nrm((DEPTH, DEC_BATCH, N_MEM, XA_HEADS, XA_HEAD_DIM)),
        'page_table': page_table,
        'mem_prompt': nrm((BATCH, N_MEM, D_MODEL)),
        'conv_norm_g': gain((nc, D_MODEL)),
        'conv_w_in': nrm((nc, D_MODEL, 3 * D_MODEL), D_MODEL ** -0.5),
        'conv_w': nrm((nc, CONV_WIDTH, D_MODEL), CONV_WIDTH ** -0.5),
        'conv_w_out': nrm((nc, D_MODEL, D_MODEL), D_MODEL ** -0.5),
        'mla_norm_g': gain((nm, D_MODEL)),
        'mla_w_down': nrm((nm, D_MODEL, Q_RANK + KV_RANK + ROPE_DIM), D_MODEL ** -0.5),
        'mla_q_norm_g': gain((nm, Q_RANK)),
        'mla_kv_norm_g': gain((nm, KV_RANK)),
        'mla_w_uq': nrm((nm, Q_RANK, MLA_HEADS, NOPE_DIM + ROPE_DIM), Q_RANK ** -0.5),
        'mla_w_uk': nrm((nm, KV_RANK, MLA_HEADS, NOPE_DIM), KV_RANK ** -0.5),
        'mla_w_uv': nrm((nm, KV_RANK, MLA_HEADS, V_DIM), KV_RANK ** -0.5),
        'mla_w_o': nrm((nm, MLA_HEADS, V_DIM, D_MODEL), (MLA_HEADS * V_DIM) ** -0.5),
        'xa_norm_g': gain((DEPTH, D_MODEL)),
        'xa_mem_norm_g': gain((DEPTH, D_MODEL)),
        'xa_w_q': nrm((DEPTH, D_MODEL, XA_HEADS, XA_HEAD_DIM), D_MODEL ** -0.5),
        'xa_w_kv': nrm((DEPTH, D_MODEL, 2, XA_HEADS, XA_HEAD_DIM), D_MODEL ** -0.5),
        'xa_w_o': nrm((DEPTH, XA_HEADS, XA_HEAD_DIM, D_MODEL), D_MODEL ** -0.5),
        'ffn_norm_g': gain((DEPTH, D_MODEL)),
        'ffn_w_up': nrm((DEPTH, D_MODEL, 2 * D_FF), D_MODEL ** -0.5),
        'ffn_conv_w': nrm((DEPTH, CONV_WIDTH, D_FF), CONV_WIDTH ** -0.5),
        'ffn_w_down': nrm((DEPTH, D_FF, D_MODEL), D_FF ** -0.5),
        'final_norm_g': gain((D_MODEL,)),
    }


def reference(x_prompt, x_sample, state_conv, cache_kv_latent, cache_k_rope, state_ffn_conv,
              cache_mem_k, cache_mem_v, page_table, mem_prompt,
              conv_norm_g, conv_w_in, conv_w, conv_w_out,
              mla_norm_g, mla_w_down, mla_q_norm_g, mla_kv_norm_g, mla_w_uq, mla_w_uk, mla_w_uv, mla_w_o,
              xa_norm_g, xa_mem_norm_g, xa_w_q, xa_w_kv, xa_w_o,
              ffn_norm_g, ffn_w_up, ffn_conv_w, ffn_w_down, final_norm_g):
    bp, sp = x_prompt.shape[0], x_prompt.shape[1]
    bs, ts = x_sample.shape[0], x_sample.shape[1]
    pos_p = jnp.arange(sp, dtype=jnp.int32)
    pos_s = PAST_LEN + jnp.arange(ts, dtype=jnp.int32)
    xp, xs = x_prompt, x_sample

    conv_p, conv_s = [], []
    lat_p, rop_p, lat_s, rop_s = [], [], [], []
    ffn_p, ffn_s = [], []
    memk_p, memv_p = [], []

    for i in range(DEPTH):
        j = i // N_MIXERS
        if i % N_MIXERS == 0:
            zero_pref = jnp.zeros((bp, CONV_WIDTH - 1, D_MODEL), xp.dtype)
            dp, npre = short_conv_mixer(rmsnorm(xp, conv_norm_g[j]), zero_pref,
                                        conv_w_in[j], conv_w[j], conv_w_out[j])
            ds, nsre = short_conv_mixer(rmsnorm(xs, conv_norm_g[j]), state_conv[j],
                                        conv_w_in[j], conv_w[j], conv_w_out[j])
            conv_p.append(npre)
            conv_s.append(nsre)
        else:
            qn, qr, ckv, kr = mla_project(rmsnorm(xp, mla_norm_g[j]), pos_p, mla_w_down[j],
                                          mla_q_norm_g[j], mla_kv_norm_g[j], mla_w_uq[j])
            k_nope = jnp.einsum('btc,chd->bthd', ckv, mla_w_uk[j])
            v = jnp.einsum('btc,chd->bthd', ckv, mla_w_uv[j])
            op = mla_prompt_attention(qn, qr, k_nope, kr, v)
            dp = jnp.einsum('bthd,hde->bte', op, mla_w_o[j])
            lat_p.append(ckv)
            rop_p.append(kr)

            qn_s, qr_s, ckv_s, kr_s = mla_project(rmsnorm(xs, mla_norm_g[j]), pos_s, mla_w_down[j],
                                                  mla_q_norm_g[j], mla_kv_norm_g[j], mla_w_uq[j])
            lat_past = cache_kv_latent[j, page_table].reshape(bs, -1, KV_RANK)
            rope_past = cache_k_rope[j, page_table].reshape(bs, -1, ROPE_DIM)
            os_ = mla_sample_attention(qn_s, qr_s, ckv_s, kr_s, lat_past, rope_past,
                                       mla_w_uk[j], mla_w_uv[j])
            ds = jnp.einsum('bthd,hde->bte', os_, mla_w_o[j])
            lat_s.append(ckv_s)
            rop_s.append(kr_s)
        xp = xp + dp
        xs = xs + ds

        mk, mv = mem_keys_values(mem_prompt, xa_mem_norm_g[i], xa_w_kv[i])
        memk_p.append(mk)
        memv_p.append(mv)
        xp = xp + mem_cross_attention(rmsnorm(xp, xa_norm_g[i]), mk, mv, xa_w_q[i], xa_w_o[i])
        xs = xs + mem_cross_attention(rmsnorm(xs, xa_norm_g[i]), cache_mem_k[i], cache_mem_v[i],
                                      xa_w_q[i], xa_w_o[i])

        zero_ff = jnp.zeros((bp, CONV_WIDTH - 1, D_FF), xp.dtype)
        fp, fpre = conv_ffn(rmsnorm(xp, ffn_norm_g[i]), zero_ff, ffn_w_up[i], ffn_conv_w[i], ffn_w_down[i])
        fs, fsre = conv_ffn(rmsnorm(xs, ffn_norm_g[i]), state_ffn_conv[i], ffn_w_up[i], ffn_conv_w[i], ffn_w_down[i])
        xp = xp + fp
        xs = xs + fs
        ffn_p.append(fpre)
        ffn_s.append(fsre)

    y_prompt = rmsnorm(xp, final_norm_g)
    y_sample = rmsnorm(xs, final_norm_g)
    return (y_prompt, y_sample,
            jnp.stack(conv_p), jnp.stack(conv_s),
            jnp.stack(lat_p), jnp.stack(rop_p), jnp.stack(lat_s), jnp.stack(rop_s),
            jnp.stack(ffn_p), jnp.stack(ffn_s),
            jnp.stack(memk_p), jnp.stack(memv_p))
```

```python
import functools

import jax
import jax.numpy as jnp
from jax import lax
from jax.experimental import pallas as pl
from jax.experimental.pallas import tpu as pltpu

F32 = jnp.float32
BF16 = jnp.bfloat16

RMS_EPS = 1e-6
ROPE_THETA = 10000.0
N_MIXERS = 2
NOPE_DIM = 128
ROPE_DIM = 64
Q_RANK = 384
KV_RANK = 256

SUBLANES = 8
LANES = 128
MXU_WIDTH = 256
VMEM_LIMIT_BYTES = 56 * 1024 * 1024

PROMPT_TILE = 512
CHUNK = MXU_WIDTH
ATTN_TQ = 512
ATTN_TK = 512
PAST_TK = 1024


def _dot(a, b):
    return jnp.dot(a, b, preferred_element_type=F32)


def _dot_nt(a, b):
    return lax.dot_general(a, b, (((1,), (1,)), ((), ())), preferred_element_type=F32)


def _rms(xf, g):
    y = xf * lax.rsqrt(jnp.mean(xf * xf, axis=-1, keepdims=True) + RMS_EPS)
    return y * g


def _params(*sem):
    return pltpu.CompilerParams(dimension_semantics=tuple(sem), vmem_limit_bytes=VMEM_LIMIT_BYTES)


def _const_spec(shape):
    nd = len(shape)
    return pl.BlockSpec(shape, lambda *_: (0,) * nd, pipeline_mode=pl.Buffered(1))


def _conv3(u3, prefix8, w, ext_ref):
    r = u3.shape[1]
    ext_ref[:, 0:SUBLANES, :] = prefix8
    ext_ref[:, SUBLANES:SUBLANES + r, :] = u3
    y = (ext_ref[:, SUBLANES - 2:SUBLANES - 2 + r, :] * w[0:1, :][None]
         + ext_ref[:, SUBLANES - 1:SUBLANES - 1 + r, :] * w[1:2, :][None]
         + u3 * w[2:3, :][None])
    return y, ext_ref[:, r:r + SUBLANES, :]


def _zero_at_sequence_start(carry_ref, tiles_per_seq):
    @pl.when(pl.program_id(0) % tiles_per_seq == 0)
    def _():
        carry_ref[...] = jnp.zeros(carry_ref.shape, carry_ref.dtype)


def _conv_mixer_kernel(*refs, groups, rows, tiles_per_seq, has_prefix):
    if has_prefix:
        x_ref, pre_ref, g_ref, win_ref, cw_ref, wout_ref, o_ref, st_ref, xn_scr, ext_scr, p_scr = refs
    else:
        x_ref, g_ref, win_ref, cw_ref, wout_ref, o_ref, st_ref, xn_scr, ext_scr, p_scr, carry_scr = refs
        _zero_at_sequence_start(carry_scr, tiles_per_seq)
    tm = groups * rows
    d = x_ref.shape[1]
    x = x_ref[...]
    xn_scr[...] = _rms(x, g_ref[...]).astype(BF16)
    for n in range(d // CHUNK):
        lo, hi = n * CHUNK, (n + 1) * CHUNK
        xn = xn_scr[...]
        b = _dot(xn, win_ref[:, lo:hi])
        c = _dot(xn, win_ref[:, d + lo:d + hi])
        h = _dot(xn, win_ref[:, 2 * d + lo:2 * d + hi])
        u3 = (c * h).reshape(groups, rows, CHUNK)
        if has_prefix:
            pre = pre_ref[:, :, lo:hi]
        else:
            pre = carry_scr[:, :, lo:hi]
        y3, new8 = _conv3(u3, pre, cw_ref[:, lo:hi], ext_scr)
        st_ref[:, :, lo:hi] = new8
        if not has_prefix:
            carry_scr[:, :, lo:hi] = new8
        p_scr[:, lo:hi] = (b * y3.reshape(tm, CHUNK)).astype(BF16)
    o_ref[...] = x + _dot(p_scr[...], wout_ref[...])


def _conv_mixer(x, prefix8, g, w_in, cw, w_out, *, seq_len):
    t, d = x.shape
    has_prefix = prefix8 is not None
    if has_prefix:
        groups, rows, tm, tps, n_state = t // SUBLANES, SUBLANES, t, 1, t // SUBLANES
    else:
        tm = min(PROMPT_TILE, seq_len)
        groups, rows, tps, n_state = 1, tm, seq_len // tm, t // seq_len
    grid = (t // tm,)
    in_specs = [pl.BlockSpec((tm, d), lambda i: (i, 0))]
    args = [x]
    if has_prefix:
        in_specs.append(pl.BlockSpec((groups, SUBLANES, d), lambda i: (0, 0, 0)))
        args.append(prefix8)
    in_specs += [_const_spec((1, d)), _const_spec(w_in.shape), _const_spec(cw.shape), _const_spec(w_out.shape)]
    args += [g.reshape(1, d), w_in, cw, w_out]
    scratch = [pltpu.VMEM((tm, d), BF16),
               pltpu.VMEM((groups, SUBLANES + rows, CHUNK), F32),
               pltpu.VMEM((tm, d), BF16)]
    if not has_prefix:
        scratch.append(pltpu.VMEM((1, SUBLANES, d), F32))
    st_block = (groups, SUBLANES, d)
    return pl.pallas_call(
        functools.partial(_conv_mixer_kernel, groups=groups, rows=rows, tiles_per_seq=tps,
                          has_prefix=has_prefix),
        grid=grid,
        in_specs=in_specs,
        out_specs=[pl.BlockSpec((tm, d), lambda i: (i, 0)),
                   pl.BlockSpec(st_block, lambda i: (i // tps, 0, 0))],
        out_shape=[jax.ShapeDtypeStruct((t, d), F32),
                   jax.ShapeDtypeStruct((n_state, SUBLANES, d), F32)],
        scratch_shapes=scratch,
        compiler_params=_params("arbitrary"),
        name="conv_mixer_sample" if has_prefix else "conv_mixer_prompt",
    )(*args)


def _ffn_kernel(*refs, groups, rows, tiles_per_seq, has_prefix, final):
    refs = list(refs)
    x_ref = refs.pop(0)
    pre_ref = refs.pop(0) if has_prefix else None
    g_ref, wg_ref, wv_ref, cw_ref, wd_ref = refs[:5]
    refs = refs[5:]
    gf_ref = refs.pop(0) if final else None
    o_ref, st_ref, xn_scr, ext_scr, h_scr = refs[:5]
    carry_scr = None if has_prefix else refs[5]
    if not has_prefix:
        _zero_at_sequence_start(carry_scr, tiles_per_seq)
    tm = groups * rows
    f = wd_ref.shape[0]
    x = x_ref[...]
    xn_scr[...] = _rms(x, g_ref[...]).astype(BF16)
    for n in range(f // CHUNK):
        lo, hi = n * CHUNK, (n + 1) * CHUNK
        xn = xn_scr[...]
        ug = _dot(xn, wg_ref[:, lo:hi])
        uv = _dot(xn, wv_ref[:, lo:hi])
        if has_prefix:
            pre = pre_ref[:, :, lo:hi]
        else:
            pre = carry_scr[:, :, lo:hi]
        gc3, new8 = _conv3(ug.reshape(groups, rows, CHUNK), pre, cw_ref[:, lo:hi], ext_scr)
        st_ref[:, :, lo:hi] = new8
        if not has_prefix:
            carry_scr[:, :, lo:hi] = new8
        gc = gc3.reshape(tm, CHUNK)
        h_scr[:, lo:hi] = ((gc * jax.nn.sigmoid(gc)) * uv).astype(BF16)
    res = x + _dot(h_scr[...], wd_ref[...])
    o_ref[...] = _rms(res, gf_ref[...]) if final else res


def _conv_ffn(x, prefix8, g, wg, wv, cw, wd, final_g, *, seq_len):
    t, d = x.shape
    f = wd.shape[0]
    has_prefix = prefix8 is not None
    final = final_g is not None
    if has_prefix:
        groups, rows, tm, tps, n_state = t // SUBLANES, SUBLANES, t, 1, t // SUBLANES
    else:
        tm = min(PROMPT_TILE, seq_len)
        groups, rows, tps, n_state = 1, tm, seq_len // tm, t // seq_len
    in_specs = [pl.BlockSpec((tm, d), lambda i: (i, 0))]
    args = [x]
    if has_prefix:
        in_specs.append(pl.BlockSpec((groups, SUBLANES, f), lambda i: (0, 0, 0)))
        args.append(prefix8)
    in_specs += [_const_spec((1, d)), _const_spec(wg.shape), _const_spec(wv.shape), _const_spec(cw.shape),
                 _const_spec(wd.shape)]
    args += [g.reshape(1, d), wg, wv, cw, wd]
    if final:
        in_specs.append(_const_spec((1, d)))
        args.append(final_g.reshape(1, d))
    scratch = [pltpu.VMEM((tm, d), BF16),
               pltpu.VMEM((groups, SUBLANES + rows, CHUNK), F32),
               pltpu.VMEM((tm, f), BF16)]
    if not has_prefix:
        scratch.append(pltpu.VMEM((1, SUBLANES, f), F32))
    return pl.pallas_call(
        functools.partial(_ffn_kernel, groups=groups, rows=rows, tiles_per_seq=tps,
                          has_prefix=has_prefix, final=final),
        grid=(t // tm,),
        in_specs=in_specs,
        out_specs=[pl.BlockSpec((tm, d), lambda i: (i, 0)),
                   pl.BlockSpec((groups, SUBLANES, f), lambda i: (i // tps, 0, 0))],
        out_shape=[jax.ShapeDtypeStruct((t, d), F32),
                   jax.ShapeDtypeStruct((n_state, SUBLANES, f), F32)],
        scratch_shapes=scratch,
        compiler_params=_params("arbitrary"),
        name="conv_ffn_sample" if has_prefix else "conv_ffn_prompt",
    )(*args)


def _mla_proj_kernel(*refs, heads, prompt):
    if prompt:
        (x_ref, g_ref, wd_ref, qg_ref, kvg_ref, wuq_ref, cos_ref, sin_ref, wuk_ref, wuv_ref,
         q_ref, ckv_ref, kr_ref, kn_ref, v_ref, krb_ref) = refs
    else:
        (x_ref, g_ref, wd_ref, qg_ref, kvg_ref, wuq_ref, cos_ref, sin_ref, wukt_ref,
         ql_ref, qr_ref, ckv_ref, kr_ref) = refs
    hn = heads * NOPE_DIM
    cos = cos_ref[...]
    sin = sin_ref[...]
    xn = _rms(x_ref[...], g_ref[...]).astype(BF16)
    dn = _dot(xn, wd_ref[...])
    c_q = _rms(dn[:, :Q_RANK], qg_ref[...]).astype(BF16)
    c_kv = _rms(dn[:, Q_RANK:Q_RANK + KV_RANK], kvg_ref[...])
    ckv_ref[...] = c_kv
    r0 = Q_RANK + KV_RANK
    k_r = dn[:, r0:r0 + LANES] * cos + dn[:, r0 + LANES:r0 + 2 * LANES] * sin
    kr_ref[...] = k_r
    q = _dot(c_q, wuq_ref[...])
    if prompt:
        q_ref[:, :hn] = q[:, :hn].astype(BF16)
    for h in range(heads):
        lo = hn + h * LANES
        qr_h = q[:, lo:lo + LANES] * cos + q[:, lo + heads * LANES:lo + (heads + 1) * LANES] * sin
        if prompt:
            q_ref[:, lo:lo + LANES] = qr_h.astype(BF16)
        else:
            qr_ref[h] = qr_h
            qn_h = q[:, h * NOPE_DIM:(h + 1) * NOPE_DIM].astype(BF16)
            ql_ref[h] = _dot(qn_h, wukt_ref[h])
    if prompt:
        ckv_b = c_kv.astype(BF16)
        kn_ref[...] = _dot(ckv_b, wuk_ref[...]).astype(BF16)
        v_ref[...] = _dot(ckv_b, wuv_ref[...]).astype(BF16)
        krb_ref[...] = k_r.astype(BF16)


def _mla_proj(x, g, wd_ext, qg, kvg, wuq_ext, cos, sin, extra, *, heads, seq_len, prompt):
    t, d = x.shape
    tm = min(PROMPT_TILE, seq_len) if prompt else t
    tps = seq_len // tm if prompt else 1
    hn = heads * NOPE_DIM
    in_specs = [pl.BlockSpec((tm, d), lambda i: (i, 0)),
                _const_spec((1, d)), _const_spec(wd_ext.shape), _const_spec((1, Q_RANK)),
                _const_spec((1, KV_RANK)), _const_spec(wuq_ext.shape),
                pl.BlockSpec((tm, LANES), lambda i: (i % tps, 0)),
                pl.BlockSpec((tm, LANES), lambda i: (i % tps, 0))]
    in_specs += [_const_spec(w.shape) for w in extra]
    row = lambda w: pl.BlockSpec((tm, w), lambda i: (i, 0))
    if prompt:
        out_specs = [row(2 * hn), row(KV_RANK), row(LANES), row(hn), row(hn), row(LANES)]
        out_shape = [jax.ShapeDtypeStruct((t, 2 * hn), BF16), jax.ShapeDtypeStruct((t, KV_RANK), F32),
                     jax.ShapeDtypeStruct((t, LANES), F32), jax.ShapeDtypeStruct((t, hn), BF16),
                     jax.ShapeDtypeStruct((t, hn), BF16), jax.ShapeDtypeStruct((t, LANES), BF16)]
    else:
        head_major = lambda w: pl.BlockSpec((heads, tm, w), lambda i: (0, i, 0))
        out_specs = [head_major(KV_RANK), head_major(LANES), row(KV_RANK), row(LANES)]
        out_shape = [jax.ShapeDtypeStruct((heads, t, KV_RANK), F32), jax.ShapeDtypeStruct((heads, t, LANES), F32),
                     jax.ShapeDtypeStruct((t, KV_RANK), F32), jax.ShapeDtypeStruct((t, LANES), F32)]
    return pl.pallas_call(
        functools.partial(_mla_proj_kernel, heads=heads, prompt=prompt),
        grid=(t // tm,),
        in_specs=in_specs,
        out_specs=out_specs,
        out_shape=out_shape,
        compiler_params=_params("arbitrary"),
        name="mla_proj_prompt" if prompt else "mla_proj_sample",
    )(x, g.reshape(1, d), wd_ext, qg.reshape(1, Q_RANK), kvg.reshape(1, KV_RANK), wuq_ext, cos, sin, *extra)


def _softmax_step(s, m, l, acc, v):
    m_new = jnp.maximum(m, jnp.max(s, axis=-1, keepdims=True))
    alpha = jnp.exp(m - m_new)
    p = jnp.exp(s - m_new)
    l_new = alpha * l + jnp.sum(p, axis=-1, keepdims=True)
    acc_new = alpha * acc + _dot(p.astype(BF16), v)
    return m_new, l_new, acc_new


def _attn_prompt_kernel(qn_ref, qr_ref, kn_ref, kr_ref, v_ref, o_ref, *, tq, tk, scale):
    i = pl.program_id(2)
    q = jnp.concatenate([qn_ref[...], qr_ref[...]], axis=1)

    def step(j, carry, masked):
        ks = pl.multiple_of(j * tk, tk)
        k = jnp.concatenate([kn_ref[pl.ds(ks, tk), :], kr_ref[pl.ds(ks, tk), :]], axis=1)
        s = _dot_nt(q, k) * scale
        if masked:
            qpos = i * tq + lax.broadcasted_iota(jnp.int32, (tq, tk), 0)
            kpos = j * tk + lax.broadcasted_iota(jnp.int32, (tq, tk), 1)
            s = jnp.where(kpos <= qpos, s, -jnp.inf)
        return _softmax_step(s, *carry, v_ref[pl.ds(ks, tk), :])

    carry = (jnp.full((tq, 1), -jnp.inf, F32), jnp.zeros((tq, 1), F32), jnp.zeros((tq, v_ref.shape[1]), F32))
    n_full = i * (tq // tk)
    carry = lax.fori_loop(0, n_full, lambda j, c: step(j, c, False), carry)
    for dj in range(tq // tk):
        carry = step(n_full + dj, carry, True)
    _, l, acc = carry
    o_ref[...] = (acc * (1.0 / l)).astype(o_ref.dtype)


def _attn_prompt(q, kn, krb, v, *, heads, seq_len, scale):
    t = q.shape[0]
    bsz = t // seq_len
    tq = min(ATTN_TQ, seq_len)
    tk = min(ATTN_TK, tq)
    nq = seq_len // tq
    return pl.pallas_call(
        functools.partial(_attn_prompt_kernel, tq=tq, tk=tk, scale=scale),
        grid=(bsz, heads, nq),
        in_specs=[pl.BlockSpec((tq, NOPE_DIM), lambda b, h, i: (b * nq + i, h)),
                  pl.BlockSpec((tq, LANES), lambda b, h, i: (b * nq + i, heads + h)),
                  pl.BlockSpec((seq_len, NOPE_DIM), lambda b, h, i: (b, h)),
                  pl.BlockSpec((seq_len, LANES), lambda b, h, i: (b, 0)),
                  pl.BlockSpec((seq_len, NOPE_DIM), lambda b, h, i: (b, h))],
        out_specs=pl.BlockSpec((tq, NOPE_DIM), lambda b, h, i: (b * nq + i, h)),
        out_shape=jax.ShapeDtypeStruct((t, heads * NOPE_DIM), BF16),
        compiler_params=_params("arbitrary", "arbitrary", "arbitrary"),
        name="mla_attn_prompt",
    )(q, q, kn, krb, v)


def _attn_sample_kernel(pt_ref, ql_ref, qr_ref, cn_ref, kn_ref, lat_hbm, rope_hbm, o_ref,
                        lat_buf, rope_buf, sem, *, layer, n_pages, page, tk, scale):
    b = pl.program_id(0)
    slot = b % 2
    heads, ts, _ = ql_ref.shape

    def page_copies(seq, p, dst_slot):
        pg = pt_ref[seq, p]
        rows = pl.ds(pl.multiple_of(p * page, page), page)
        return (pltpu.make_async_copy(lat_hbm.at[layer, pg], lat_buf.at[dst_slot, rows, :], sem.at[0, dst_slot]),
                pltpu.make_async_copy(rope_hbm.at[layer, pg], rope_buf.at[dst_slot, rows, :], sem.at[1, dst_slot]))

    def start_fetch(seq, dst_slot):
        def body(p, _):
            for cp in page_copies(seq, p, dst_slot):
                cp.start()
            return 0
        lax.fori_loop(0, n_pages, body, 0)

    @pl.when(b == 0)
    def _():
        start_fetch(0, 0)

    @pl.when(b + 1 < pl.num_programs(0))
    def _():
        start_fetch(b + 1, 1 - slot)

    def wait_body(p, _):
        for cp in page_copies(b, p, slot):
            cp.wait()
        return 0
    lax.fori_loop(0, n_pages, wait_body, 0)

    ql = ql_ref[...].reshape(heads * ts, KV_RANK).astype(BF16)
    qr = qr_ref[...].reshape(heads * ts, LANES)[:, :ROPE_DIM].astype(BF16)

    def chunk(c, carry):
        ks = pl.multiple_of(c * tk, tk)
        latc = lat_buf[slot, pl.ds(ks, tk), :].astype(BF16)
        ropc = rope_buf[slot, pl.ds(ks, tk), :].astype(BF16)
        s = (_dot_nt(ql, latc) + _dot_nt(qr, ropc)) * scale
        return _softmax_step(s, *carry, latc)

    rows = heads * ts
    carry = (jnp.full((rows, 1), -jnp.inf, F32), jnp.zeros((rows, 1), F32), jnp.zeros((rows, KV_RANK), F32))
    carry = lax.fori_loop(0, (n_pages * page) // tk, chunk, carry)

    pad = 2 * SUBLANES - ts
    cn = jnp.concatenate([cn_ref[...], jnp.zeros((pad, KV_RANK), F32)], axis=0).astype(BF16)
    kn = jnp.concatenate([kn_ref[...][:, :ROPE_DIM], jnp.zeros((pad, ROPE_DIM), F32)], axis=0).astype(BF16)
    s = (_dot_nt(ql, cn) + _dot_nt(qr, kn)) * scale
    t_q = lax.broadcasted_iota(jnp.int32, s.shape, 0) % ts
    t_k = lax.broadcasted_iota(jnp.int32, s.shape, 1)
    s = jnp.where(t_k <= t_q, s, -jnp.inf)
    _, l, acc = _softmax_step(s, *carry, cn)
    o_ref[...] = (acc * (1.0 / l)).reshape(heads, ts, KV_RANK)


def _attn_sample(page_table, ql, qr, ckv, kr, cache_lat, cache_rope, *, layer, ts, scale):
    heads, t, _ = ql.shape
    n_seq, n_pages = page_table.shape
    page = cache_lat.shape[2]
    n_past = n_pages * page
    tk = min(PAST_TK, n_past)
    grid_spec = pltpu.PrefetchScalarGridSpec(
        num_scalar_prefetch=1,
        grid=(n_seq,),
        in_specs=[pl.BlockSpec((heads, ts, KV_RANK), lambda b, pt: (0, b, 0)),
                  pl.BlockSpec((heads, ts, LANES), lambda b, pt: (0, b, 0)),
                  pl.BlockSpec((ts, KV_RANK), lambda b, pt: (b, 0)),
                  pl.BlockSpec((ts, LANES), lambda b, pt: (b, 0)),
                  pl.BlockSpec(memory_space=pl.ANY),
                  pl.BlockSpec(memory_space=pl.ANY)],
        out_specs=pl.BlockSpec((heads, ts, KV_RANK), lambda b, pt: (0, b, 0)),
        scratch_shapes=[pltpu.VMEM((2, n_past, KV_RANK), F32),
                        pltpu.VMEM((2, n_past, ROPE_DIM), F32),
                        pltpu.SemaphoreType.DMA((2, 2))],
    )
    return pl.pallas_call(
        functools.partial(_attn_sample_kernel, layer=layer, n_pages=n_pages, page=page, tk=tk, scale=scale),
        grid_spec=grid_spec,
        out_shape=jax.ShapeDtypeStruct((heads, t, KV_RANK), F32),
        compiler_params=_params("arbitrary"),
        name="mla_attn_sample",
    )(page_table, ql, qr, ckv, kr, cache_lat, cache_rope)


def _mla_sample_out_kernel(ol_ref, wuv_ref, wo_ref, x_ref, o_ref, os_scr):
    heads = ol_ref.shape[0]
    vd = wuv_ref.shape[2]
    for h in range(heads):
        os_scr[:, h * vd:(h + 1) * vd] = _dot(ol_ref[h].astype(BF16), wuv_ref[h]).astype(BF16)
    o_ref[...] = x_ref[...] + _dot(os_scr[...], wo_ref[...])


def _mla_sample_out(ol, wuv_h, wo, x):
    t, d = x.shape
    heads, _, vd = wuv_h.shape
    return pl.pallas_call(
        _mla_sample_out_kernel,
        grid=(1,),
        in_specs=[_const_spec(ol.shape), _const_spec(wuv_h.shape), _const_spec(wo.shape), _const_spec(x.shape)],
        out_specs=pl.BlockSpec((t, d), lambda i: (0, 0)),
        out_shape=jax.ShapeDtypeStruct((t, d), F32),
        scratch_shapes=[pltpu.VMEM((t, heads * vd), BF16)],
        compiler_params=_params("arbitrary"),
        name="mla_out_sample",
    )(ol, wuv_h, wo, x)


def _matmul_residual_kernel(a_ref, w_ref, x_ref, o_ref):
    o_ref[...] = x_ref[...] + _dot(a_ref[...].astype(BF16), w_ref[...])


def _matmul_residual(a, w, x, *, tm):
    t, d = x.shape
    k = a.shape[1]
    return pl.pallas_call(
        _matmul_residual_kernel,
        grid=(t // tm,),
        in_specs=[pl.BlockSpec((tm, k), lambda i: (i, 0)), _const_spec(w.shape),
                  pl.BlockSpec((tm, d), lambda i: (i, 0))],
        out_specs=pl.BlockSpec((tm, d), lambda i: (i, 0)),
        out_shape=jax.ShapeDtypeStruct((t, d), F32),
        compiler_params=_params("arbitrary"),
        name="matmul_residual",
    )(a, w, x)


def _mem_kv_kernel(mem_ref, g_ref, wk_ref, wv_ref, k_ref, v_ref, mn_scr):
    @pl.when(pl.program_id(1) == 0)
    def _():
        mn_scr[...] = _rms(mem_ref[...], g_ref[...]).astype(BF16)
    mn = mn_scr[...]
    k_ref[...] = _dot(mn, wk_ref[...])
    v_ref[...] = _dot(mn, wv_ref[...])


def _mem_kv(mem, g, wk, wv, *, tn=512):
    m, d = mem.shape
    depth, _, e = wk.shape
    return pl.pallas_call(
        _mem_kv_kernel,
        grid=(depth, e // tn),
        in_specs=[_const_spec((m, d)),
                  pl.BlockSpec((None, 1, d), lambda l, n: (l, 0, 0)),
                  pl.BlockSpec((None, d, tn), lambda l, n: (l, 0, n)),
                  pl.BlockSpec((None, d, tn), lambda l, n: (l, 0, n))],
        out_specs=[pl.BlockSpec((None, m, tn), lambda l, n: (l, 0, n)),
                   pl.BlockSpec((None, m, tn), lambda l, n: (l, 0, n))],
        out_shape=[jax.ShapeDtypeStruct((depth, m, e), F32), jax.ShapeDtypeStruct((depth, m, e), F32)],
        scratch_shapes=[pltpu.VMEM((m, d), BF16)],
        compiler_params=_params("arbitrary", "arbitrary"),
        name="mem_kv",
    )(mem, g.reshape(depth, 1, d), wk, wv)


def _xattn_heads(q, k_of, v_of, heads, scale):
    e = q.shape[1] // heads
    outs = []
    for h in range(heads):
        s = _dot_nt(q[:, h * e:(h + 1) * e].astype(BF16), k_of(h)) * scale
        p = jnp.exp(s - jnp.max(s, axis=-1, keepdims=True))
        p = p * (1.0 / jnp.sum(p, axis=-1, keepdims=True))
        outs.append(_dot(p.astype(BF16), v_of(h)))
    return outs


def _xattn_prompt_kernel(x_ref, g_ref, wq_ref, mk_ref, mv_ref, wo_ref, o_ref, o_scr, *, heads, scale):
    x = x_ref[...]
    q = _dot(_rms(x, g_ref[...]).astype(BF16), wq_ref[...])
    e = q.shape[1] // heads
    outs = _xattn_heads(q, lambda h: mk_ref[:, h * e:(h + 1) * e].astype(BF16),
                        lambda h: mv_ref[:, h * e:(h + 1) * e].astype(BF16), heads, scale)
    for h in range(heads):
        o_scr[:, h * e:(h + 1) * e] = outs[h].astype(BF16)
    o_ref[...] = x + _dot(o_scr[...], wo_ref[...])


def _xattn_prompt(x, g, wq, mk_all, mv_all, wo, *, layer, heads, seq_len, n_mem, scale):
    t, d = x.shape
    tm = min(PROMPT_TILE, seq_len)
    tps = seq_len // tm
    return pl.pallas_call(
        functools.partial(_xattn_prompt_kernel, heads=heads, scale=scale),
        grid=(t // tm,),
        in_specs=[pl.BlockSpec((tm, d), lambda i: (i, 0)),
                  _const_spec((1, d)), _const_spec(wq.shape),
                  pl.BlockSpec((None, n_mem, d), lambda i: (layer, i // tps, 0)),
                  pl.BlockSpec((None, n_mem, d), lambda i: (layer, i // tps, 0)),
                  _const_spec(wo.shape)],
        out_specs=pl.BlockSpec((tm, d), lambda i: (i, 0)),
        out_shape=jax.ShapeDtypeStruct((t, d), F32),
        scratch_shapes=[pltpu.VMEM((tm, d), BF16)],
        compiler_params=_params("arbitrary"),
        name="xattn_prompt",
    )(x, g.reshape(1, d), wq, mk_all, mv_all, wo)


def _norm_matmul_kernel(x_ref, g_ref, w_ref, o_ref):
    o_ref[...] = _dot(_rms(x_ref[...], g_ref[...]).astype(BF16), w_ref[...])


def _norm_matmul(x, g, w):
    t, d = x.shape
    n = w.shape[1]
    return pl.pallas_call(
        _norm_matmul_kernel,
        grid=(1,),
        in_specs=[_const_spec((t, d)), _const_spec((1, d)), _const_spec(w.shape)],
        out_specs=pl.BlockSpec((t, n), lambda i: (0, 0)),
        out_shape=jax.ShapeDtypeStruct((t, n), F32),
        compiler_params=_params("arbitrary"),
        name="norm_matmul",
    )(x, g.reshape(1, d), w)


def _xattn_sample_kernel(q_ref, mk_ref, mv_ref, o_ref, *, heads, ts, scale):
    nseq = mk_ref.shape[0]
    e = q_ref.shape[1] // heads
    for s_i in range(nseq):
        q = q_ref[s_i * ts:(s_i + 1) * ts, :]
        outs = _xattn_heads(q, lambda h: mk_ref[s_i, :, h * e:(h + 1) * e].astype(BF16),
                            lambda h: mv_ref[s_i, :, h * e:(h + 1) * e].astype(BF16), heads, scale)
        for h in range(heads):
            o_ref[s_i * ts:(s_i + 1) * ts, h * e:(h + 1) * e] = outs[h]


def _xattn_sample(q, mk_all, mv_all, *, layer, heads, ts, scale, seqs_per_step=4):
    t, d = q.shape
    _, n_seq, n_mem, _ = mk_all.shape
    sp = min(seqs_per_step, n_seq)
    return pl.pallas_call(
        functools.partial(_xattn_sample_kernel, heads=heads, ts=ts, scale=scale),
        grid=(n_seq // sp,),
        in_specs=[pl.BlockSpec((sp * ts, d), lambda i: (i, 0)),
                  pl.BlockSpec((None, sp, n_mem, d), lambda i: (layer, i, 0, 0)),
                  pl.BlockSpec((None, sp, n_mem, d), lambda i: (layer, i, 0, 0))],
        out_specs=pl.BlockSpec((sp * ts, d), lambda i: (i, 0)),
        out_shape=jax.ShapeDtypeStruct((t, d), F32),
        compiler_params=_params("arbitrary"),
        name="xattn_sample",
    )(q, mk_all, mv_all)


def _rope_tables(pos):
    half = ROPE_DIM // 2
    inv = 1.0 / (ROPE_THETA ** (jnp.arange(half, dtype=F32) * (2.0 / ROPE_DIM)))
    ang = pos.astype(F32)[:, None] * inv[None, :]
    reps = LANES // half
    return jnp.tile(jnp.cos(ang), (1, reps)), jnp.tile(jnp.sin(ang), (1, reps))


def _rot_cols(w):
    half = w.shape[-1] // 2
    return jnp.concatenate([-w[..., half:], w[..., :half]], axis=-1)


def _pad_lanes(w):
    return jnp.pad(w, [(0, 0)] * (w.ndim - 1) + [(0, LANES - w.shape[-1])])


def _state_to_prefix8(state):
    return jnp.pad(state, ((0, 0), (SUBLANES - state.shape[1], 0), (0, 0)))


def kernel(x_prompt, x_sample, state_conv, cache_kv_latent, cache_k_rope, state_ffn_conv, cache_mem_k, cache_mem_v, page_table, mem_prompt, conv_norm_g, conv_w_in, conv_w, conv_w_out, mla_norm_g, mla_w_down, mla_q_norm_g, mla_kv_norm_g, mla_w_uq, mla_w_uk, mla_w_uv, mla_w_o, xa_norm_g, xa_mem_norm_g, xa_w_q, xa_w_kv, xa_w_o, ffn_norm_g, ffn_w_up, ffn_conv_w, ffn_w_down, final_norm_g):
    bp, sp, d = x_prompt.shape
    bs, ts, _ = x_sample.shape
    depth = ffn_w_up.shape[0]
    d_ff = ffn_w_down.shape[1]
    heads = mla_w_uq.shape[2]
    xa_heads, xa_dim = xa_w_q.shape[2], xa_w_q.shape[3]
    n_mem = mem_prompt.shape[1]
    n_pages, page = page_table.shape[1], cache_kv_latent.shape[2]
    score_scale = (NOPE_DIM + ROPE_DIM) ** -0.5
    xa_scale = xa_dim ** -0.5
    assert ts == SUBLANES and KV_RANK + Q_RANK + ROPE_DIM == mla_w_down.shape[2]

    xp = x_prompt.reshape(bp * sp, d)
    xs = x_sample.reshape(bs * ts, d)

    cos_p, sin_p = _rope_tables(jnp.arange(sp, dtype=jnp.int32))
    cos_s, sin_s = _rope_tables(n_pages * page + jnp.arange(ts, dtype=jnp.int32))
    cos_s, sin_s = jnp.tile(cos_s, (bs, 1)), jnp.tile(sin_s, (bs, 1))

    wkv = xa_w_kv.reshape(depth, d, 2, xa_heads * xa_dim).astype(BF16)
    mk_all, mv_all = _mem_kv(mem_prompt.reshape(bp * n_mem, d), xa_mem_norm_g, wkv[:, :, 0], wkv[:, :, 1])
    cmk = cache_mem_k.reshape(depth, bs, n_mem, xa_heads * xa_dim)
    cmv = cache_mem_v.reshape(depth, bs, n_mem, xa_heads * xa_dim)

    conv_p, conv_s, lat_p, rop_p, lat_s, rop_s, ffn_p, ffn_s = [], [], [], [], [], [], [], []

    for i in range(depth):
        j = i // N_MIXERS
        if i % N_MIXERS == 0:
            w_in, w_out = conv_w_in[j].astype(BF16), conv_w_out[j].astype(BF16)
            xp, st = _conv_mixer(xp, None, conv_norm_g[j], w_in, conv_w[j], w_out, seq_len=sp)
            conv_p.append(st[:, SUBLANES - 2:, :])
            xs, st = _conv_mixer(xs, _state_to_prefix8(state_conv[j]), conv_norm_g[j], w_in, conv_w[j], w_out,
                                 seq_len=ts)
            conv_s.append(st[:, SUBLANES - 2:, :])
        else:
            wd = mla_w_down[j]
            w_r = wd[:, Q_RANK + KV_RANK:]
            wd_ext = jnp.concatenate([wd[:, :Q_RANK + KV_RANK], _pad_lanes(w_r), _pad_lanes(_rot_cols(w_r))],
                                     axis=1).astype(BF16)
            wuq = mla_w_uq[j]
            wuq_r = wuq[:, :, NOPE_DIM:]
            wuq_ext = jnp.concatenate([wuq[:, :, :NOPE_DIM].reshape(Q_RANK, heads * NOPE_DIM),
                                       _pad_lanes(wuq_r).reshape(Q_RANK, heads * LANES),
                                       _pad_lanes(_rot_cols(wuq_r)).reshape(Q_RANK, heads * LANES)],
                                      axis=1).astype(BF16)
            wuk = mla_w_uk[j].astype(BF16)
            wuv = mla_w_uv[j].astype(BF16)
            wo = mla_w_o[j].reshape(heads * NOPE_DIM, d).astype(BF16)

            q, ckv, kr, kn, v, krb = _mla_proj(
                xp, mla_norm_g[j], wd_ext, mla_q_norm_g[j], mla_kv_norm_g[j], wuq_ext, cos_p, sin_p,
                [wuk.reshape(KV_RANK, heads * NOPE_DIM), wuv.reshape(KV_RANK, heads * NOPE_DIM)],
                heads=heads, seq_len=sp, prompt=True)
            op = _attn_prompt(q, kn, krb, v, heads=heads, seq_len=sp, scale=score_scale)
            xp = _matmul_residual(op, wo, xp, tm=min(PROMPT_TILE, sp))
            lat_p.append(ckv.reshape(bp, sp, KV_RANK))
            rop_p.append(kr[:, :ROPE_DIM].reshape(bp, sp, ROPE_DIM))

            ql, qr, ckv_s, kr_s = _mla_proj(
                xs, mla_norm_g[j], wd_ext, mla_q_norm_g[j], mla_kv_norm_g[j], wuq_ext, cos_s, sin_s,
                [wuk.transpose(1, 2, 0)], heads=heads, seq_len=ts, prompt=False)
            ol = _attn_sample(page_table, ql, qr, ckv_s, kr_s, cache_kv_latent, cache_k_rope,
                              layer=j, ts=ts, scale=score_scale)
            xs = _mla_sample_out(ol, wuv.transpose(1, 0, 2), wo, xs)
            lat_s.append(ckv_s.reshape(bs, ts, KV_RANK))
            rop_s.append(kr_s[:, :ROPE_DIM].reshape(bs, ts, ROPE_DIM))

        wq = xa_w_q[i].reshape(d, xa_heads * xa_dim).astype(BF16)
        wo_x = xa_w_o[i].reshape(xa_heads * xa_dim, d).astype(BF16)
        xp = _xattn_prompt(xp, xa_norm_g[i], wq, mk_all, mv_all, wo_x, layer=i, heads=xa_heads, seq_len=sp,
                           n_mem=n_mem, scale=xa_scale)
        qs = _norm_matmul(xs, xa_norm_g[i], wq)
        os_ = _xattn_sample(qs, cmk, cmv, layer=i, heads=xa_heads, ts=ts, scale=xa_scale)
        xs = _matmul_residual(os_, wo_x, xs, tm=bs * ts)

        w_up = ffn_w_up[i].astype(BF16)
        wg, wv_ = w_up[:, :d_ff], w_up[:, d_ff:]
        wdn = ffn_w_down[i].astype(BF16)
        fg = final_norm_g if i == depth - 1 else None
        xp, st = _conv_ffn(xp, None, ffn_norm_g[i], wg, wv_, ffn_conv_w[i], wdn, fg, seq_len=sp)
        ffn_p.append(st[:, SUBLANES - 2:, :])
        xs, st = _conv_ffn(xs, _state_to_prefix8(state_ffn_conv[i]), ffn_norm_g[i], wg, wv_, ffn_conv_w[i], wdn,
                           fg, seq_len=ts)
        ffn_s.append(st[:, SUBLANES - 2:, :])

    mem_shape = (depth, bp, n_mem, xa_heads, xa_dim)
    return (xp.reshape(bp, sp, d), xs.reshape(bs, ts, d),
            jnp.stack(conv_p), jnp.stack(conv_s),
            jnp.stack(lat_p), jnp.stack(rop_p), jnp.stack(lat_s), jnp.stack(rop_s),
            jnp.stack(ffn_p), jnp.stack(ffn_s),
            mk_all.reshape(mem_shape), mv_all.reshape(mem_shape))
```

```python
import functools
import math

import jax
import jax.numpy as jnp
from jax import lax
from jax.experimental import pallas as pl
from jax.experimental.pallas import tpu as pltpu

F32 = jnp.float32
BF16 = jnp.bfloat16

RMS_EPS = 1e-6
ROPE_THETA = 10000.0
N_MIXERS = 2
NOPE_DIM = 128
ROPE_DIM = 64
Q_RANK = 384
KV_RANK = 256

SUBLANES = 8
LANES = 128
MXU_WIDTH = 256
VMEM_LIMIT_BYTES = 56 * 1024 * 1024

PROMPT_TILE = 512
CHUNK = MXU_WIDTH
ATTN_TILE = 512
PAST_CHUNK_PAGES = 8
LOG2_E = math.log2(math.e)


def _dot(a, b):
    return jnp.dot(a, b, preferred_element_type=F32)


def _dot_nt(a, b):
    return lax.dot_general(a, b, (((1,), (1,)), ((), ())), preferred_element_type=F32)


def _rms(xf, g):
    y = xf * lax.rsqrt(jnp.mean(xf * xf, axis=-1, keepdims=True) + RMS_EPS)
    return y * g


def _params(*sem):
    return pltpu.CompilerParams(dimension_semantics=tuple(sem), vmem_limit_bytes=VMEM_LIMIT_BYTES)


def _const_spec(shape):
    nd = len(shape)
    return pl.BlockSpec(shape, lambda *_: (0,) * nd, pipeline_mode=pl.Buffered(1))


def _conv3(u3, prefix8, w, ext_ref):
    r = u3.shape[1]
    ext_ref[:, 0:SUBLANES, :] = prefix8
    ext_ref[:, SUBLANES:SUBLANES + r, :] = u3
    y = (ext_ref[:, SUBLANES - 2:SUBLANES - 2 + r, :] * w[0:1, :][None]
         + ext_ref[:, SUBLANES - 1:SUBLANES - 1 + r, :] * w[1:2, :][None]
         + u3 * w[2:3, :][None])
    return y, ext_ref[:, r:r + SUBLANES, :]


def _zero_at_sequence_start(carry_ref, tiles_per_seq):
    @pl.when(pl.program_id(0) % tiles_per_seq == 0)
    def _():
        carry_ref[...] = jnp.zeros(carry_ref.shape, carry_ref.dtype)


def _conv_mixer_kernel(*refs, groups, rows, tiles_per_seq, has_prefix):
    if has_prefix:
        x_ref, pre_ref, g_ref, win_ref, cw_ref, wout_ref, o_ref, st_ref, xn_scr, ext_scr, p_scr = refs
    else:
        x_ref, g_ref, win_ref, cw_ref, wout_ref, o_ref, st_ref, xn_scr, ext_scr, p_scr, carry_scr = refs
        _zero_at_sequence_start(carry_scr, tiles_per_seq)
    tm = groups * rows
    d = x_ref.shape[1]
    x = x_ref[...]
    xn_scr[...] = _rms(x, g_ref[...]).astype(BF16)
    for n in range(d // CHUNK):
        lo, hi = n * CHUNK, (n + 1) * CHUNK
        xn = xn_scr[...]
        b = _dot(xn, win_ref[:, lo:hi])
        c = _dot(xn, win_ref[:, d + lo:d + hi])
        h = _dot(xn, win_ref[:, 2 * d + lo:2 * d + hi])
        u3 = (c * h).reshape(groups, rows, CHUNK)
        if has_prefix:
            pre = pre_ref[:, :, lo:hi]
        else:
            pre = carry_scr[:, :, lo:hi]
        y3, new8 = _conv3(u3, pre, cw_ref[:, lo:hi], ext_scr)
        st_ref[:, :, lo:hi] = new8
        if not has_prefix:
            carry_scr[:, :, lo:hi] = new8
        p_scr[:, lo:hi] = (b * y3.reshape(tm, CHUNK)).astype(BF16)
    o_ref[...] = x + _dot(p_scr[...], wout_ref[...])


def _conv_mixer(x, prefix8, g, w_in, cw, w_out, *, seq_len):
    t, d = x.shape
    has_prefix = prefix8 is not None
    if has_prefix:
        groups, rows, tm, tps, n_state = t // SUBLANES, SUBLANES, t, 1, t // SUBLANES
    else:
        tm = min(PROMPT_TILE, seq_len)
        groups, rows, tps, n_state = 1, tm, seq_len // tm, t // seq_len
    grid = (t // tm,)
    in_specs = [pl.BlockSpec((tm, d), lambda i: (i, 0))]
    args = [x]
    if has_prefix:
        in_specs.append(pl.BlockSpec((groups, SUBLANES, d), lambda i: (0, 0, 0)))
        args.append(prefix8)
    in_specs += [_const_spec((1, d)), _const_spec(w_in.shape), _const_spec(cw.shape), _const_spec(w_out.shape)]
    args += [g.reshape(1, d), w_in, cw, w_out]
    scratch = [pltpu.VMEM((tm, d), BF16),
               pltpu.VMEM((groups, SUBLANES + rows, CHUNK), F32),
               pltpu.VMEM((tm, d), BF16)]
    if not has_prefix:
        scratch.append(pltpu.VMEM((1, SUBLANES, d), F32))
    st_block = (groups, SUBLANES, d)
    return pl.pallas_call(
        functools.partial(_conv_mixer_kernel, groups=groups, rows=rows, tiles_per_seq=tps,
                          has_prefix=has_prefix),
        grid=grid,
        in_specs=in_specs,
        out_specs=[pl.BlockSpec((tm, d), lambda i: (i, 0)),
                   pl.BlockSpec(st_block, lambda i: (i // tps, 0, 0))],
        out_shape=[jax.ShapeDtypeStruct((t, d), F32),
                   jax.ShapeDtypeStruct((n_state, SUBLANES, d), F32)],
        scratch_shapes=scratch,
        compiler_params=_params("arbitrary"),
        name="conv_mixer_sample" if has_prefix else "conv_mixer_prompt",
    )(*args)


def _ffn_kernel(*refs, groups, rows, tiles_per_seq, has_prefix, final):
    refs = list(refs)
    x_ref = refs.pop(0)
    pre_ref = refs.pop(0) if has_prefix else None
    g_ref, wg_ref, wv_ref, cw_ref, wd_ref = refs[:5]
    refs = refs[5:]
    gf_ref = refs.pop(0) if final else None
    o_ref, st_ref, xn_scr, ext_scr, h_scr = refs[:5]
    carry_scr = None if has_prefix else refs[5]
    if not has_prefix:
        _zero_at_sequence_start(carry_scr, tiles_per_seq)
    tm = groups * rows
    f = wd_ref.shape[0]
    x = x_ref[...]
    xn_scr[...] = _rms(x, g_ref[...]).astype(BF16)
    for n in range(f // CHUNK):
        lo, hi = n * CHUNK, (n + 1) * CHUNK
        xn = xn_scr[...]
        ug = _dot(xn, wg_ref[:, lo:hi])
        uv = _dot(xn, wv_ref[:, lo:hi])
        if has_prefix:
            pre = pre_ref[:, :, lo:hi]
        else:
            pre = carry_scr[:, :, lo:hi]
        gc3, new8 = _conv3(ug.reshape(groups, rows, CHUNK), pre, cw_ref[:, lo:hi], ext_scr)
        st_ref[:, :, lo:hi] = new8
        if not has_prefix:
            carry_scr[:, :, lo:hi] = new8
        gc = gc3.reshape(tm, CHUNK)
        h_scr[:, lo:hi] = ((gc * jax.nn.sigmoid(gc)) * uv).astype(BF16)
    res = x + _dot(h_scr[...], wd_ref[...])
    o_ref[...] = _rms(res, gf_ref[...]) if final else res


def _conv_ffn(x, prefix8, g, wg, wv, cw, wd, final_g, *, seq_len):
    t, d = x.shape
    f = wd.shape[0]
    has_prefix = prefix8 is not None
    final = final_g is not None
    if has_prefix:
        groups, rows, tm, tps, n_state = t // SUBLANES, SUBLANES, t, 1, t // SUBLANES
    else:
        tm = min(PROMPT_TILE, seq_len)
        groups, rows, tps, n_state = 1, tm, seq_len // tm, t // seq_len
    in_specs = [pl.BlockSpec((tm, d), lambda i: (i, 0))]
    args = [x]
    if has_prefix:
        in_specs.append(pl.BlockSpec((groups, SUBLANES, f), lambda i: (0, 0, 0)))
        args.append(prefix8)
    in_specs += [_const_spec((1, d)), _const_spec(wg.shape), _const_spec(wv.shape), _const_spec(cw.shape),
                 _const_spec(wd.shape)]
    args += [g.reshape(1, d), wg, wv, cw, wd]
    if final:
        in_specs.append(_const_spec((1, d)))
        args.append(final_g.reshape(1, d))
    scratch = [pltpu.VMEM((tm, d), BF16),
               pltpu.VMEM((groups, SUBLANES + rows, CHUNK), F32),
               pltpu.VMEM((tm, f), BF16)]
    if not has_prefix:
        scratch.append(pltpu.VMEM((1, SUBLANES, f), F32))
    return pl.pallas_call(
        functools.partial(_ffn_kernel, groups=groups, rows=rows, tiles_per_seq=tps,
                          has_prefix=has_prefix, final=final),
        grid=(t // tm,),
        in_specs=in_specs,
        out_specs=[pl.BlockSpec((tm, d), lambda i: (i, 0)),
                   pl.BlockSpec((groups, SUBLANES, f), lambda i: (i // tps, 0, 0))],
        out_shape=[jax.ShapeDtypeStruct((t, d), F32),
                   jax.ShapeDtypeStruct((n_state, SUBLANES, f), F32)],
        scratch_shapes=scratch,
        compiler_params=_params("arbitrary"),
        name="conv_ffn_sample" if has_prefix else "conv_ffn_prompt",
    )(*args)


def _mla_proj_kernel(*refs, heads, prompt):
    if prompt:
        (x_ref, g_ref, wd_ref, qg_ref, kvg_ref, wuq_ref, cos_ref, sin_ref, wuk_ref, wuvt_ref,
         q_ref, ckv_ref, kr_ref, kn_ref, vt_ref, krb_ref) = refs
    else:
        (x_ref, g_ref, wd_ref, qg_ref, kvg_ref, wuq_ref, cos_ref, sin_ref, wukt_ref,
         ql_ref, qr_ref, ckv_ref, kr_ref) = refs
    hn = heads * NOPE_DIM
    cos = cos_ref[...]
    sin = sin_ref[...]
    xn = _rms(x_ref[...], g_ref[...]).astype(BF16)
    dn = _dot(xn, wd_ref[...])
    c_q = _rms(dn[:, :Q_RANK], qg_ref[...]).astype(BF16)
    c_kv = _rms(dn[:, Q_RANK:Q_RANK + KV_RANK], kvg_ref[...])
    ckv_ref[...] = c_kv
    r0 = Q_RANK + KV_RANK
    k_r = dn[:, r0:r0 + LANES] * cos + dn[:, r0 + LANES:r0 + 2 * LANES] * sin
    kr_ref[...] = k_r
    q = _dot(c_q, wuq_ref[...])
    if prompt:
        q_ref[:, :hn] = q[:, :hn].astype(BF16)
    for h in range(heads):
        lo = hn + h * LANES
        qr_h = q[:, lo:lo + LANES] * cos + q[:, lo + heads * LANES:lo + (heads + 1) * LANES] * sin
        if prompt:
            q_ref[:, lo:lo + LANES] = qr_h.astype(BF16)
        else:
            qr_ref[h] = qr_h
            qn_h = q[:, h * NOPE_DIM:(h + 1) * NOPE_DIM].astype(BF16)
            ql_ref[h] = _dot(qn_h, wukt_ref[h])
    if prompt:
        ckv_b = c_kv.astype(BF16)
        kn_ref[...] = _dot(ckv_b, wuk_ref[...]).astype(BF16)
        vt_ref[...] = _dot_nt(wuvt_ref[...], ckv_b).astype(BF16)
        krb_ref[...] = k_r.astype(BF16)


def _mla_proj(x, g, wd_ext, qg, kvg, wuq_ext, cos, sin, extra, *, heads, seq_len, prompt):
    t, d = x.shape
    tm = min(PROMPT_TILE, seq_len) if prompt else t
    tps = seq_len // tm if prompt else 1
    hn = heads * NOPE_DIM
    in_specs = [pl.BlockSpec((tm, d), lambda i: (i, 0)),
                _const_spec((1, d)), _const_spec(wd_ext.shape), _const_spec((1, Q_RANK)),
                _const_spec((1, KV_RANK)), _const_spec(wuq_ext.shape),
                pl.BlockSpec((tm, LANES), lambda i: (i % tps, 0)),
                pl.BlockSpec((tm, LANES), lambda i: (i % tps, 0))]
    in_specs += [_const_spec(w.shape) for w in extra]
    row = lambda w: pl.BlockSpec((tm, w), lambda i: (i, 0))
    if prompt:
        out_specs = [row(2 * hn), row(KV_RANK), row(LANES), row(hn),
                     pl.BlockSpec((hn, tm), lambda i: (0, i)), row(LANES)]
        out_shape = [jax.ShapeDtypeStruct((t, 2 * hn), BF16), jax.ShapeDtypeStruct((t, KV_RANK), F32),
                     jax.ShapeDtypeStruct((t, LANES), F32), jax.ShapeDtypeStruct((t, hn), BF16),
                     jax.ShapeDtypeStruct((hn, t), BF16), jax.ShapeDtypeStruct((t, LANES), BF16)]
    else:
        head_major = lambda w: pl.BlockSpec((heads, tm, w), lambda i: (0, i, 0))
        out_specs = [head_major(KV_RANK), head_major(LANES), row(KV_RANK), row(LANES)]
        out_shape = [jax.ShapeDtypeStruct((heads, t, KV_RANK), F32), jax.ShapeDtypeStruct((heads, t, LANES), F32),
                     jax.ShapeDtypeStruct((t, KV_RANK), F32), jax.ShapeDtypeStruct((t, LANES), F32)]
    return pl.pallas_call(
        functools.partial(_mla_proj_kernel, heads=heads, prompt=prompt),
        grid=(t // tm,),
        in_specs=in_specs,
        out_specs=out_specs,
        out_shape=out_shape,
        compiler_params=_params("arbitrary"),
        name="mla_proj_prompt" if prompt else "mla_proj_sample",
    )(x, g.reshape(1, d), wd_ext, qg.reshape(1, Q_RANK), kvg.reshape(1, KV_RANK), wuq_ext, cos, sin, *extra)


def _attn_prompt_kernel(qn_ref, qr_ref, kn_ref, kr_ref, vt_ref, o_ref,
                        sa_scr, sb_scr, m_scr, l_scr, acc_scr, *, t, scale):
    i = pl.program_id(2)
    q = jnp.concatenate([qn_ref[...], qr_ref[...]], axis=1)
    c = scale * LOG2_E
    m_scr[...] = jnp.full(m_scr.shape, -jnp.inf, F32)
    l_scr[...] = jnp.zeros(l_scr.shape, F32)
    acc_scr[...] = jnp.zeros(acc_scr.shape, F32)

    def scores(j, dst):
        ks = pl.multiple_of(j * t, t)
        k = jnp.concatenate([kn_ref[pl.ds(ks, t), :], kr_ref[pl.ds(ks, t), :]], axis=1)
        dst[...] = _dot_nt(k, q)

    def update(src, j, diagonal):
        ks = pl.multiple_of(j * t, t)
        s = src[...]
        if diagonal:
            kpos = lax.broadcasted_iota(jnp.int32, (t, t), 0)
            qpos = lax.broadcasted_iota(jnp.int32, (t, t), 1)
            s = jnp.where(kpos <= qpos, s, -jnp.inf)
        m = m_scr[...]
        m_new = jnp.maximum(m, jnp.max(s, axis=0, keepdims=True))
        alpha = jnp.exp2((m - m_new) * c)
        p = jnp.exp2((s - m_new) * c)
        l_scr[...] = alpha * l_scr[...] + jnp.sum(p, axis=0, keepdims=True)
        acc_scr[...] = alpha * acc_scr[...] + _dot(vt_ref[:, pl.ds(ks, t)], p.astype(BF16))
        m_scr[...] = m_new

    scores(0, sa_scr)

    def pair(jj, _):
        scores(2 * jj + 1, sb_scr)
        update(sa_scr, 2 * jj, False)
        scores(2 * jj + 2, sa_scr)
        update(sb_scr, 2 * jj + 1, False)
        return 0
    lax.fori_loop(0, i // 2, pair, 0)

    @pl.when(i % 2 == 0)
    def _():
        update(sa_scr, i, True)

    @pl.when(i % 2 == 1)
    def _():
        scores(i, sb_scr)
        update(sa_scr, i - 1, False)
        update(sb_scr, i, True)

    o_ref[...] = (acc_scr[...] * (1.0 / l_scr[...])).T.astype(o_ref.dtype)


def _attn_prompt(q, kn, krb, vt, *, heads, seq_len, scale):
    tokens = q.shape[0]
    bsz = tokens // seq_len
    t = min(ATTN_TILE, seq_len)
    nq = seq_len // t
    return pl.pallas_call(
        functools.partial(_attn_prompt_kernel, t=t, scale=scale),
        grid=(bsz, heads, nq),
        in_specs=[pl.BlockSpec((t, NOPE_DIM), lambda b, h, i: (b * nq + i, h)),
                  pl.BlockSpec((t, LANES), lambda b, h, i: (b * nq + i, heads + h)),
                  pl.BlockSpec((seq_len, NOPE_DIM), lambda b, h, i: (b, h)),
                  pl.BlockSpec((seq_len, LANES), lambda b, h, i: (b, 0)),
                  pl.BlockSpec((NOPE_DIM, seq_len), lambda b, h, i: (h, b))],
        out_specs=pl.BlockSpec((t, NOPE_DIM), lambda b, h, i: (b * nq + i, h)),
        out_shape=jax.ShapeDtypeStruct((tokens, heads * NOPE_DIM), BF16),
        scratch_shapes=[pltpu.VMEM((t, t), F32), pltpu.VMEM((t, t), F32), pltpu.VMEM((1, t), F32),
                        pltpu.VMEM((1, t), F32), pltpu.VMEM((NOPE_DIM, t), F32)],
        compiler_params=_params("arbitrary", "arbitrary", "arbitrary"),
        name="mla_attn_prompt",
    )(q, q, kn, krb, vt)


def _attn_sample_kernel(pt_ref, ql_ref, qr_ref, cn_ref, kn_ref, lat_hbm, ropet_hbm, o_ref,
                        lat_buf, rope_buf, s_scr, sem, *, layer, n_pages, page, ppc, scale):
    b = pl.program_id(0)
    nb = pl.num_programs(0)
    slot = b % 2
    heads, ts, _ = ql_ref.shape
    tk = ppc * page
    n_chunks = n_pages // ppc

    def page_copies(seq, p, dst_slot):
        pg = pt_ref[seq, p]
        return (pltpu.make_async_copy(lat_hbm.at[layer, pg], lat_buf.at[dst_slot, p * page:(p + 1) * page, :],
                                      sem.at[dst_slot]),
                pltpu.make_async_copy(ropet_hbm.at[layer, pg], rope_buf.at[dst_slot, p], sem.at[dst_slot]))

    def fetch(seq, dst_slot, wait):
        for p in range(n_pages):
            for cp in page_copies(seq, p, dst_slot):
                cp.wait() if wait else cp.start()

    @pl.when(b == 0)
    def _():
        fetch(0, 0, False)

    fetch(b, slot, True)
    fetch((b + 1) % nb, 1 - slot, False)

    ql = ql_ref[...].reshape(heads * ts, KV_RANK).astype(BF16)
    qr = qr_ref[...].reshape(heads * ts, LANES)[:, :ROPE_DIM].astype(BF16)

    def lat_chunk(c):
        return lat_buf[slot, c * tk:(c + 1) * tk, :].astype(BF16)

    for c in range(n_chunks):
        ropt = jnp.concatenate([rope_buf[slot, c * ppc + k] for k in range(ppc)], axis=1).astype(BF16)
        s_scr[:, c * tk:(c + 1) * tk] = (_dot_nt(ql, lat_chunk(c)) + _dot(qr, ropt)) * scale

    pad = 2 * SUBLANES - ts
    cn = jnp.concatenate([cn_ref[...], jnp.zeros((pad, KV_RANK), F32)], axis=0).astype(BF16)
    kn = jnp.concatenate([kn_ref[...][:, :ROPE_DIM], jnp.zeros((pad, ROPE_DIM), F32)], axis=0).astype(BF16)
    s_new = (_dot_nt(ql, cn) + _dot_nt(qr, kn)) * scale
    t_q = lax.broadcasted_iota(jnp.int32, s_new.shape, 0) % ts
    t_k = lax.broadcasted_iota(jnp.int32, s_new.shape, 1)
    s_new = jnp.where(t_k <= t_q, s_new, -jnp.inf)

    m = jnp.maximum(jnp.max(s_scr[...], axis=-1, keepdims=True), jnp.max(s_new, axis=-1, keepdims=True))
    p_new = jnp.exp(s_new - m)
    l = jnp.sum(p_new, axis=-1, keepdims=True)
    acc = _dot(p_new.astype(BF16), cn)
    for c in range(n_chunks):
        p = jnp.exp(s_scr[:, c * tk:(c + 1) * tk] - m)
        l = l + jnp.sum(p, axis=-1, keepdims=True)
        acc = acc + _dot(p.astype(BF16), lat_chunk(c))
    o_ref[...] = (acc * (1.0 / l)).reshape(heads, ts, KV_RANK)

    @pl.when(b == nb - 1)
    def _():
        fetch(0, 1 - slot, True)


def _attn_sample(page_table, ql, qr, ckv, kr, cache_lat, cache_rope_t, *, layer, ts, scale):
    heads, t, _ = ql.shape
    n_seq, n_pages = page_table.shape
    page = cache_lat.shape[2]
    n_past = n_pages * page
    ppc = min(PAST_CHUNK_PAGES, n_pages)
    grid_spec = pltpu.PrefetchScalarGridSpec(
        num_scalar_prefetch=1,
        grid=(n_seq,),
        in_specs=[pl.BlockSpec((heads, ts, KV_RANK), lambda b, pt: (0, b, 0)),
                  pl.BlockSpec((heads, ts, LANES), lambda b, pt: (0, b, 0)),
                  pl.BlockSpec((ts, KV_RANK), lambda b, pt: (b, 0)),
                  pl.BlockSpec((ts, LANES), lambda b, pt: (b, 0)),
                  pl.BlockSpec(memory_space=pl.ANY),
                  pl.BlockSpec(memory_space=pl.ANY)],
        out_specs=pl.BlockSpec((heads, ts, KV_RANK), lambda b, pt: (0, b, 0)),
        scratch_shapes=[pltpu.VMEM((2, n_past, KV_RANK), F32),
                        pltpu.VMEM((2, n_pages, ROPE_DIM, page), F32),
                        pltpu.VMEM((heads * ts, n_past), F32),
                        pltpu.SemaphoreType.DMA((2,))],
    )
    return pl.pallas_call(
        functools.partial(_attn_sample_kernel, layer=layer, n_pages=n_pages, page=page, ppc=ppc, scale=scale),
        grid_spec=grid_spec,
        out_shape=jax.ShapeDtypeStruct((heads, t, KV_RANK), F32),
        compiler_params=_params("arbitrary"),
        name="mla_attn_sample",
    )(page_table, ql, qr, ckv, kr, cache_lat, cache_rope_t)


def _mla_sample_out_kernel(ol_ref, wuv_ref, wo_ref, x_ref, o_ref, os_scr):
    heads = ol_ref.shape[0]
    vd = wuv_ref.shape[2]
    for h in range(heads):
        os_scr[:, h * vd:(h + 1) * vd] = _dot(ol_ref[h].astype(BF16), wuv_ref[h]).astype(BF16)
    o_ref[...] = x_ref[...] + _dot(os_scr[...], wo_ref[...])


def _mla_sample_out(ol, wuv_h, wo, x):
    t, d = x.shape
    heads, _, vd = wuv_h.shape
    return pl.pallas_call(
        _mla_sample_out_kernel,
        grid=(1,),
        in_specs=[_const_spec(ol.shape), _const_spec(wuv_h.shape), _const_spec(wo.shape), _const_spec(x.shape)],
        out_specs=pl.BlockSpec((t, d), lambda i: (0, 0)),
        out_shape=jax.ShapeDtypeStruct((t, d), F32),
        scratch_shapes=[pltpu.VMEM((t, heads * vd), BF16)],
        compiler_params=_params("arbitrary"),
        name="mla_out_sample",
    )(ol, wuv_h, wo, x)


def _matmul_residual_kernel(a_ref, w_ref, x_ref, o_ref):
    o_ref[...] = x_ref[...] + _dot(a_ref[...].astype(BF16), w_ref[...])


def _matmul_residual(a, w, x, *, tm):
    t, d = x.shape
    k = a.shape[1]
    return pl.pallas_call(
        _matmul_residual_kernel,
        grid=(t // tm,),
        in_specs=[pl.BlockSpec((tm, k), lambda i: (i, 0)), _const_spec(w.shape),
                  pl.BlockSpec((tm, d), lambda i: (i, 0))],
        out_specs=pl.BlockSpec((tm, d), lambda i: (i, 0)),
        out_shape=jax.ShapeDtypeStruct((t, d), F32),
        compiler_params=_params("arbitrary"),
        name="matmul_residual",
    )(a, w, x)


def _mem_kv_kernel(mem_ref, g_ref, wk_ref, wv_ref, k_ref, v_ref, mn_scr):
    @pl.when(pl.program_id(1) == 0)
    def _():
        mn_scr[...] = _rms(mem_ref[...], g_ref[...]).astype(BF16)
    mn = mn_scr[...]
    k_ref[...] = _dot(mn, wk_ref[...])
    v_ref[...] = _dot(mn, wv_ref[...])


def _mem_kv(mem, g, wk, wv, *, tn=512):
    m, d = mem.shape
    depth, _, e = wk.shape
    return pl.pallas_call(
        _mem_kv_kernel,
        grid=(depth, e // tn),
        in_specs=[_const_spec((m, d)),
                  pl.BlockSpec((None, 1, d), lambda l, n: (l, 0, 0)),
                  pl.BlockSpec((None, d, tn), lambda l, n: (l, 0, n)),
                  pl.BlockSpec((None, d, tn), lambda l, n: (l, 0, n))],
        out_specs=[pl.BlockSpec((None, m, tn), lambda l, n: (l, 0, n)),
                   pl.BlockSpec((None, m, tn), lambda l, n: (l, 0, n))],
        out_shape=[jax.ShapeDtypeStruct((depth, m, e), F32), jax.ShapeDtypeStruct((depth, m, e), F32)],
        scratch_shapes=[pltpu.VMEM((m, d), BF16)],
        compiler_params=_params("arbitrary", "arbitrary"),
        name="mem_kv",
    )(mem, g.reshape(depth, 1, d), wk, wv)


def _xattn_heads(q, k_of, v_of, heads, scale):
    e = q.shape[1] // heads
    outs = []
    for h in range(heads):
        s = _dot_nt(q[:, h * e:(h + 1) * e].astype(BF16), k_of(h)) * scale
        p = jnp.exp(s - jnp.max(s, axis=-1, keepdims=True))
        p = p * (1.0 / jnp.sum(p, axis=-1, keepdims=True))
        outs.append(_dot(p.astype(BF16), v_of(h)))
    return outs


def _xattn_prompt_kernel(x_ref, g_ref, wq_ref, mk_ref, mv_ref, wo_ref, o_ref, o_scr, *, heads, scale):
    x = x_ref[...]
    q = _dot(_rms(x, g_ref[...]).astype(BF16), wq_ref[...])
    e = q.shape[1] // heads
    outs = _xattn_heads(q, lambda h: mk_ref[:, h * e:(h + 1) * e].astype(BF16),
                        lambda h: mv_ref[:, h * e:(h + 1) * e].astype(BF16), heads, scale)
    for h in range(heads):
        o_scr[:, h * e:(h + 1) * e] = outs[h].astype(BF16)
    o_ref[...] = x + _dot(o_scr[...], wo_ref[...])


def _xattn_prompt(x, g, wq, mk_all, mv_all, wo, *, layer, heads, seq_len, n_mem, scale):
    t, d = x.shape
    tm = min(PROMPT_TILE, seq_len)
    tps = seq_len // tm
    return pl.pallas_call(
        functools.partial(_xattn_prompt_kernel, heads=heads, scale=scale),
        grid=(t // tm,),
        in_specs=[pl.BlockSpec((tm, d), lambda i: (i, 0)),
                  _const_spec((1, d)), _const_spec(wq.shape),
                  pl.BlockSpec((None, n_mem, d), lambda i: (layer, i // tps, 0)),
                  pl.BlockSpec((None, n_mem, d), lambda i: (layer, i // tps, 0)),
                  _const_spec(wo.shape)],
        out_specs=pl.BlockSpec((tm, d), lambda i: (i, 0)),
        out_shape=jax.ShapeDtypeStruct((t, d), F32),
        scratch_shapes=[pltpu.VMEM((tm, d), BF16)],
        compiler_params=_params("arbitrary"),
        name="xattn_prompt",
    )(x, g.reshape(1, d), wq, mk_all, mv_all, wo)


def _norm_matmul_kernel(x_ref, g_ref, w_ref, o_ref):
    o_ref[...] = _dot(_rms(x_ref[...], g_ref[...]).astype(BF16), w_ref[...])


def _norm_matmul(x, g, w):
    t, d = x.shape
    n = w.shape[1]
    return pl.pallas_call(
        _norm_matmul_kernel,
        grid=(1,),
        in_specs=[_const_spec((t, d)), _const_spec((1, d)), _const_spec(w.shape)],
        out_specs=pl.BlockSpec((t, n), lambda i: (0, 0)),
        out_shape=jax.ShapeDtypeStruct((t, n), F32),
        compiler_params=_params("arbitrary"),
        name="norm_matmul",
    )(x, g.reshape(1, d), w)


def _xattn_sample_kernel(q_ref, mk_ref, mv_ref, o_ref, *, ts, scale):
    nseq, _, heads, e = mk_ref.shape
    for s_i in range(nseq):
        q = q_ref[s_i * ts:(s_i + 1) * ts, :]
        q_hm = jnp.concatenate([q[:, h * e:(h + 1) * e] for h in range(heads)], axis=0).astype(BF16)
        k_all = mk_ref[s_i].reshape(-1, e).astype(BF16)
        v_all = mv_ref[s_i].reshape(-1, e).astype(BF16)
        s = _dot_nt(q_hm, k_all) * scale
        key_head = lax.broadcasted_iota(jnp.int32, s.shape, 1) % heads
        row_head = lax.broadcasted_iota(jnp.int32, s.shape, 0) // ts
        s = jnp.where(key_head == row_head, s, -jnp.inf)
        p = jnp.exp(s - jnp.max(s, axis=-1, keepdims=True))
        p = p * (1.0 / jnp.sum(p, axis=-1, keepdims=True))
        o = _dot(p.astype(BF16), v_all)
        for h in range(heads):
            o_ref[s_i * ts:(s_i + 1) * ts, h * e:(h + 1) * e] = o[h * ts:(h + 1) * ts]


def _xattn_sample(q, mk_all, mv_all, *, layer, ts, scale, seqs_per_step=4):
    t, d = q.shape
    _, n_seq, n_mem, heads, e = mk_all.shape
    sp = min(seqs_per_step, n_seq)
    cache_spec = pl.BlockSpec((None, sp, n_mem, heads, e), lambda i: (layer, i, 0, 0, 0))
    return pl.pallas_call(
        functools.partial(_xattn_sample_kernel, ts=ts, scale=scale),
        grid=(n_seq // sp,),
        in_specs=[pl.BlockSpec((sp * ts, d), lambda i: (i, 0)), cache_spec, cache_spec],
        out_specs=pl.BlockSpec((sp * ts, d), lambda i: (i, 0)),
        out_shape=jax.ShapeDtypeStruct((t, d), F32),
        compiler_params=_params("arbitrary"),
        name="xattn_sample",
    )(q, mk_all, mv_all)


def _rope_tables(pos):
    half = ROPE_DIM // 2
    inv = 1.0 / (ROPE_THETA ** (jnp.arange(half, dtype=F32) * (2.0 / ROPE_DIM)))
    ang = pos.astype(F32)[:, None] * inv[None, :]
    reps = LANES // half
    return jnp.tile(jnp.cos(ang), (1, reps)), jnp.tile(jnp.sin(ang), (1, reps))


def _rot_cols(w):
    half = w.shape[-1] // 2
    return jnp.concatenate([-w[..., half:], w[..., :half]], axis=-1)


def _pad_lanes(w):
    return jnp.pad(w, [(0, 0)] * (w.ndim - 1) + [(0, LANES - w.shape[-1])])


def _state_to_prefix8(state):
    return jnp.pad(state, ((0, 0), (SUBLANES - state.shape[1], 0), (0, 0)))


def kernel(x_prompt, x_sample, state_conv, cache_kv_latent, cache_k_rope, state_ffn_conv, cache_mem_k, cache_mem_v, page_table, mem_prompt, conv_norm_g, conv_w_in, conv_w, conv_w_out, mla_norm_g, mla_w_down, mla_q_norm_g, mla_kv_norm_g, mla_w_uq, mla_w_uk, mla_w_uv, mla_w_o, xa_norm_g, xa_mem_norm_g, xa_w_q, xa_w_kv, xa_w_o, ffn_norm_g, ffn_w_up, ffn_conv_w, ffn_w_down, final_norm_g):
    bp, sp, d = x_prompt.shape
    bs, ts, _ = x_sample.shape
    depth = ffn_w_up.shape[0]
    d_ff = ffn_w_down.shape[1]
    heads = mla_w_uq.shape[2]
    xa_heads, xa_dim = xa_w_q.shape[2], xa_w_q.shape[3]
    n_mem = mem_prompt.shape[1]
    n_pages, page = page_table.shape[1], cache_kv_latent.shape[2]
    score_scale = (NOPE_DIM + ROPE_DIM) ** -0.5
    xa_scale = xa_dim ** -0.5
    assert ts == SUBLANES and KV_RANK + Q_RANK + ROPE_DIM == mla_w_down.shape[2]

    xp = x_prompt.reshape(bp * sp, d)
    xs = x_sample.reshape(bs * ts, d)

    cos_p, sin_p = _rope_tables(jnp.arange(sp, dtype=jnp.int32))
    cos_s, sin_s = _rope_tables(n_pages * page + jnp.arange(ts, dtype=jnp.int32))
    cos_s, sin_s = jnp.tile(cos_s, (bs, 1)), jnp.tile(sin_s, (bs, 1))

    wkv = xa_w_kv.reshape(depth, d, 2, xa_heads * xa_dim).astype(BF16)
    mk_all, mv_all = _mem_kv(mem_prompt.reshape(bp * n_mem, d), xa_mem_norm_g, wkv[:, :, 0], wkv[:, :, 1])
    cache_rope_t = jnp.swapaxes(cache_k_rope, 2, 3)

    conv_p, conv_s, lat_p, rop_p, lat_s, rop_s, ffn_p, ffn_s = [], [], [], [], [], [], [], []

    for i in range(depth):
        j = i // N_MIXERS
        if i % N_MIXERS == 0:
            w_in, w_out = conv_w_in[j].astype(BF16), conv_w_out[j].astype(BF16)
            xp, st = _conv_mixer(xp, None, conv_norm_g[j], w_in, conv_w[j], w_out, seq_len=sp)
            conv_p.append(st[:, SUBLANES - 2:, :])
            xs, st = _conv_mixer(xs, _state_to_prefix8(state_conv[j]), conv_norm_g[j], w_in, conv_w[j], w_out,
                                 seq_len=ts)
            conv_s.append(st[:, SUBLANES - 2:, :])
        else:
            wd = mla_w_down[j]
            w_r = wd[:, Q_RANK + KV_RANK:]
            wd_ext = jnp.concatenate([wd[:, :Q_RANK + KV_RANK], _pad_lanes(w_r), _pad_lanes(_rot_cols(w_r))],
                                     axis=1).astype(BF16)
            wuq = mla_w_uq[j]
            wuq_r = wuq[:, :, NOPE_DIM:]
            wuq_ext = jnp.concatenate([wuq[:, :, :NOPE_DIM].reshape(Q_RANK, heads * NOPE_DIM),
                                       _pad_lanes(wuq_r).reshape(Q_RANK, heads * LANES),
                                       _pad_lanes(_rot_cols(wuq_r)).reshape(Q_RANK, heads * LANES)],
                                      axis=1).astype(BF16)
            wuk = mla_w_uk[j].astype(BF16)
            wuv = mla_w_uv[j].astype(BF16)
            wo = mla_w_o[j].reshape(heads * NOPE_DIM, d).astype(BF16)

            q, ckv, kr, kn, vt, krb = _mla_proj(
                xp, mla_norm_g[j], wd_ext, mla_q_norm_g[j], mla_kv_norm_g[j], wuq_ext, cos_p, sin_p,
                [wuk.reshape(KV_RANK, heads * NOPE_DIM), wuv.reshape(KV_RANK, heads * NOPE_DIM).T],
                heads=heads, seq_len=sp, prompt=True)
            op = _attn_prompt(q, kn, krb, vt, heads=heads, seq_len=sp, scale=score_scale)
            xp = _matmul_residual(op, wo, xp, tm=min(PROMPT_TILE, sp))
            lat_p.append(ckv.reshape(bp, sp, KV_RANK))
            rop_p.append(kr[:, :ROPE_DIM].reshape(bp, sp, ROPE_DIM))

            ql, qr, ckv_s, kr_s = _mla_proj(
                xs, mla_norm_g[j], wd_ext, mla_q_norm_g[j], mla_kv_norm_g[j], wuq_ext, cos_s, sin_s,
                [wuk.transpose(1, 2, 0)], heads=heads, seq_len=ts, prompt=False)
            ol = _attn_sample(page_table, ql, qr, ckv_s, kr_s, cache_kv_latent, cache_rope_t,
                              layer=j, ts=ts, scale=score_scale)
            xs = _mla_sample_out(ol, wuv.transpose(1, 0, 2), wo, xs)
            lat_s.append(ckv_s.reshape(bs, ts, KV_RANK))
            rop_s.append(kr_s[:, :ROPE_DIM].reshape(bs, ts, ROPE_DIM))

        wq = xa_w_q[i].reshape(d, xa_heads * xa_dim).astype(BF16)
        wo_x = xa_w_o[i].reshape(xa_heads * xa_dim, d).astype(BF16)
        xp = _xattn_prompt(xp, xa_norm_g[i], wq, mk_all, mv_all, wo_x, layer=i, heads=xa_heads, seq_len=sp,
                           n_mem=n_mem, scale=xa_scale)
        qs = _norm_matmul(xs, xa_norm_g[i], wq)
        os_ = _xattn_sample(qs, cache_mem_k, cache_mem_v, layer=i, ts=ts, scale=xa_scale)
        xs = _matmul_residual(os_, wo_x, xs, tm=bs * ts)

        w_up = ffn_w_up[i].astype(BF16)
        wg, wv_ = w_up[:, :d_ff], w_up[:, d_ff:]
        wdn = ffn_w_down[i].astype(BF16)
        fg = final_norm_g if i == depth - 1 else None
        xp, st = _conv_ffn(xp, None, ffn_norm_g[i], wg, wv_, ffn_conv_w[i], wdn, fg, seq_len=sp)
        ffn_p.append(st[:, SUBLANES - 2:, :])
        xs, st = _conv_ffn(xs, _state_to_prefix8(state_ffn_conv[i]), ffn_norm_g[i], wg, wv_, ffn_conv_w[i], wdn,
                           fg, seq_len=ts)
        ffn_s.append(st[:, SUBLANES - 2:, :])

    mem_shape = (depth, bp, n_mem, xa_heads, xa_dim)
    return (xp.reshape(bp, sp, d), xs.reshape(bs, ts, d),
            jnp.stack(conv_p), jnp.stack(conv_s),
            jnp.stack(lat_p), jnp.stack(rop_p), jnp.stack(lat_s), jnp.stack(rop_s),
            jnp.stack(ffn_p), jnp.stack(ffn_s),
            mk_all.reshape(mem_shape), mv_all.reshape(mem_shape))
```

```python
import functools
import math

import jax
import jax.numpy as jnp
from jax import lax
from jax.experimental import pallas as pl
from jax.experimental.pallas import tpu as pltpu

F32 = jnp.float32
BF16 = jnp.bfloat16

RMS_EPS = 1e-6
ROPE_THETA = 10000.0
N_MIXERS = 2
NOPE_DIM = 128
ROPE_DIM = 64
Q_RANK = 384
KV_RANK = 256

SUBLANES = 8
LANES = 128
MXU_WIDTH = 256
VMEM_LIMIT_BYTES = 56 * 1024 * 1024

PROMPT_TILE = 512
CHUNK = MXU_WIDTH
ATTN_TILE = 512
ATTN_HEADS_PER_STEP = 2
PAST_CHUNK_PAGES = 8
LOG2_E = math.log2(math.e)


def _dot(a, b):
    return jnp.dot(a, b, preferred_element_type=F32)


def _dot_nt(a, b):
    return lax.dot_general(a, b, (((1,), (1,)), ((), ())), preferred_element_type=F32)


def _rms(xf, g):
    y = xf * lax.rsqrt(jnp.mean(xf * xf, axis=-1, keepdims=True) + RMS_EPS)
    return y * g


def _params(*sem):
    return pltpu.CompilerParams(dimension_semantics=tuple(sem), vmem_limit_bytes=VMEM_LIMIT_BYTES)


def _const_spec(shape):
    nd = len(shape)
    return pl.BlockSpec(shape, lambda *_: (0,) * nd, pipeline_mode=pl.Buffered(1))


def _weight_spec(w, layer=None):
    if layer is None:
        return _const_spec(w.shape)
    nd = w.ndim - 1
    return pl.BlockSpec((None,) + w.shape[1:], lambda *_: (layer,) + (0,) * nd, pipeline_mode=pl.Buffered(1))


def _conv3(u3, prefix8, w, ext_ref):
    r = u3.shape[1]
    ext_ref[:, 0:SUBLANES, :] = prefix8
    ext_ref[:, SUBLANES:SUBLANES + r, :] = u3
    y = (ext_ref[:, SUBLANES - 2:SUBLANES - 2 + r, :] * w[0:1, :][None]
         + ext_ref[:, SUBLANES - 1:SUBLANES - 1 + r, :] * w[1:2, :][None]
         + u3 * w[2:3, :][None])
    return y, ext_ref[:, r:r + SUBLANES, :]


def _zero_at_sequence_start(carry_ref, tiles_per_seq):
    @pl.when(pl.program_id(0) % tiles_per_seq == 0)
    def _():
        carry_ref[...] = jnp.zeros(carry_ref.shape, carry_ref.dtype)


def _conv_mixer_kernel(*refs, groups, rows, tiles_per_seq, has_prefix):
    if has_prefix:
        x_ref, pre_ref, g_ref, win_ref, cw_ref, wout_ref, o_ref, st_ref, xn_scr, ext_scr, p_scr = refs
    else:
        x_ref, g_ref, win_ref, cw_ref, wout_ref, o_ref, st_ref, xn_scr, ext_scr, p_scr, carry_scr = refs
        _zero_at_sequence_start(carry_scr, tiles_per_seq)
    tm = groups * rows
    d = x_ref.shape[1]
    x = x_ref[...]
    xn_scr[...] = _rms(x, g_ref[...]).astype(BF16)
    for n in range(d // CHUNK):
        lo, hi = n * CHUNK, (n + 1) * CHUNK
        xn = xn_scr[...]
        b = _dot(xn, win_ref[:, lo:hi])
        c = _dot(xn, win_ref[:, d + lo:d + hi])
        h = _dot(xn, win_ref[:, 2 * d + lo:2 * d + hi])
        u3 = (c * h).reshape(groups, rows, CHUNK)
        if has_prefix:
            pre = pre_ref[:, :, lo:hi]
        else:
            pre = carry_scr[:, :, lo:hi]
        y3, new8 = _conv3(u3, pre, cw_ref[:, lo:hi], ext_scr)
        st_ref[:, :, lo:hi] = new8
        if not has_prefix:
            carry_scr[:, :, lo:hi] = new8
        p_scr[:, lo:hi] = (b * y3.reshape(tm, CHUNK)).astype(BF16)
    o_ref[...] = x + _dot(p_scr[...], wout_ref[...])


def _conv_mixer(x, prefix8, g, w_in, cw, w_out, *, layer, seq_len):
    t, d = x.shape
    has_prefix = prefix8 is not None
    if has_prefix:
        groups, rows, tm, tps, n_state = t // SUBLANES, SUBLANES, t, 1, t // SUBLANES
    else:
        tm = min(PROMPT_TILE, seq_len)
        groups, rows, tps, n_state = 1, tm, seq_len // tm, t // seq_len
    grid = (t // tm,)
    in_specs = [pl.BlockSpec((tm, d), lambda i: (i, 0))]
    args = [x]
    if has_prefix:
        in_specs.append(pl.BlockSpec((groups, SUBLANES, d), lambda i: (0, 0, 0)))
        args.append(prefix8)
    in_specs += [_weight_spec(w, layer) for w in (g, w_in, cw, w_out)]
    args += [g, w_in, cw, w_out]
    scratch = [pltpu.VMEM((tm, d), BF16),
               pltpu.VMEM((groups, SUBLANES + rows, CHUNK), F32),
               pltpu.VMEM((tm, d), BF16)]
    if not has_prefix:
        scratch.append(pltpu.VMEM((1, SUBLANES, d), F32))
    st_block = (groups, SUBLANES, d)
    return pl.pallas_call(
        functools.partial(_conv_mixer_kernel, groups=groups, rows=rows, tiles_per_seq=tps,
                          has_prefix=has_prefix),
        grid=grid,
        in_specs=in_specs,
        out_specs=[pl.BlockSpec((tm, d), lambda i: (i, 0)),
                   pl.BlockSpec(st_block, lambda i: (i // tps, 0, 0))],
        out_shape=[jax.ShapeDtypeStruct((t, d), F32),
                   jax.ShapeDtypeStruct((n_state, SUBLANES, d), F32)],
        scratch_shapes=scratch,
        compiler_params=_params("arbitrary"),
        name="conv_mixer_sample" if has_prefix else "conv_mixer_prompt",
    )(*args)


def _ffn_kernel(*refs, groups, rows, tiles_per_seq, has_prefix, final):
    refs = list(refs)
    x_ref = refs.pop(0)
    pre_ref = refs.pop(0) if has_prefix else None
    g_ref, wup_ref, cw_ref, wd_ref = refs[:4]
    refs = refs[4:]
    gf_ref = refs.pop(0) if final else None
    o_ref, st_ref, xn_scr, ext_scr, h_scr = refs[:5]
    carry_scr = None if has_prefix else refs[5]
    if not has_prefix:
        _zero_at_sequence_start(carry_scr, tiles_per_seq)
    tm = groups * rows
    f = wd_ref.shape[0]
    x = x_ref[...]
    xn_scr[...] = _rms(x, g_ref[...]).astype(BF16)
    for n in range(f // CHUNK):
        lo, hi = n * CHUNK, (n + 1) * CHUNK
        xn = xn_scr[...]
        ug = _dot(xn, wup_ref[:, lo:hi])
        uv = _dot(xn, wup_ref[:, f + lo:f + hi])
        if has_prefix:
            pre = pre_ref[:, :, lo:hi]
        else:
            pre = carry_scr[:, :, lo:hi]
        gc3, new8 = _conv3(ug.reshape(groups, rows, CHUNK), pre, cw_ref[:, lo:hi], ext_scr)
        st_ref[:, :, lo:hi] = new8
        if not has_prefix:
            carry_scr[:, :, lo:hi] = new8
        gc = gc3.reshape(tm, CHUNK)
        h_scr[:, lo:hi] = ((gc * jax.nn.sigmoid(gc)) * uv).astype(BF16)
    res = x + _dot(h_scr[...], wd_ref[...])
    o_ref[...] = _rms(res, gf_ref[...]) if final else res


def _conv_ffn(x, prefix8, g, w_up, cw, wd, final_g, *, layer, seq_len):
    t, d = x.shape
    f = wd.shape[1]
    has_prefix = prefix8 is not None
    final = final_g is not None
    if has_prefix:
        groups, rows, tm, tps, n_state = t // SUBLANES, SUBLANES, t, 1, t // SUBLANES
    else:
        tm = min(PROMPT_TILE, seq_len)
        groups, rows, tps, n_state = 1, tm, seq_len // tm, t // seq_len
    in_specs = [pl.BlockSpec((tm, d), lambda i: (i, 0))]
    args = [x]
    if has_prefix:
        in_specs.append(pl.BlockSpec((groups, SUBLANES, f), lambda i: (0, 0, 0)))
        args.append(prefix8)
    in_specs += [_weight_spec(w, layer) for w in (g, w_up, cw, wd)]
    args += [g, w_up, cw, wd]
    if final:
        in_specs.append(_const_spec((1, d)))
        args.append(final_g.reshape(1, d))
    scratch = [pltpu.VMEM((tm, d), BF16),
               pltpu.VMEM((groups, SUBLANES + rows, CHUNK), F32),
               pltpu.VMEM((tm, f), BF16)]
    if not has_prefix:
        scratch.append(pltpu.VMEM((1, SUBLANES, f), F32))
    return pl.pallas_call(
        functools.partial(_ffn_kernel, groups=groups, rows=rows, tiles_per_seq=tps,
                          has_prefix=has_prefix, final=final),
        grid=(t // tm,),
        in_specs=in_specs,
        out_specs=[pl.BlockSpec((tm, d), lambda i: (i, 0)),
                   pl.BlockSpec((groups, SUBLANES, f), lambda i: (i // tps, 0, 0))],
        out_shape=[jax.ShapeDtypeStruct((t, d), F32),
                   jax.ShapeDtypeStruct((n_state, SUBLANES, f), F32)],
        scratch_shapes=scratch,
        compiler_params=_params("arbitrary"),
        name="conv_ffn_sample" if has_prefix else "conv_ffn_prompt",
    )(*args)


def _mla_proj_kernel(*refs, heads, prompt):
    if prompt:
        (x_ref, g_ref, wd_ref, qg_ref, kvg_ref, wuq_ref, cos_ref, sin_ref, wuk_ref, wuvt_ref,
         q_ref, ckv_ref, kr_ref, kn_ref, vt_ref, krb_ref) = refs
    else:
        (x_ref, g_ref, wd_ref, qg_ref, kvg_ref, wuq_ref, cos_ref, sin_ref, wukt_ref,
         ql_ref, qr_ref, ckv_ref, kr_ref) = refs
    hn = heads * NOPE_DIM
    cos = cos_ref[...]
    sin = sin_ref[...]
    xn = _rms(x_ref[...], g_ref[...]).astype(BF16)
    dn = _dot(xn, wd_ref[...])
    c_q = _rms(dn[:, :Q_RANK], qg_ref[...]).astype(BF16)
    c_kv = _rms(dn[:, Q_RANK:Q_RANK + KV_RANK], kvg_ref[...])
    ckv_ref[...] = c_kv
    r0 = Q_RANK + KV_RANK
    k_r = dn[:, r0:r0 + LANES] * cos + dn[:, r0 + LANES:r0 + 2 * LANES] * sin
    kr_ref[...] = k_r
    q = _dot(c_q, wuq_ref[...])
    if prompt:
        q_ref[:, :hn] = q[:, :hn].astype(BF16)
    for h in range(heads):
        lo = hn + h * LANES
        qr_h = q[:, lo:lo + LANES] * cos + q[:, lo + heads * LANES:lo + (heads + 1) * LANES] * sin
        if prompt:
            q_ref[:, lo:lo + LANES] = qr_h.astype(BF16)
        else:
            qr_ref[h] = qr_h
            qn_h = q[:, h * NOPE_DIM:(h + 1) * NOPE_DIM].astype(BF16)
            ql_ref[h] = _dot(qn_h, wukt_ref[h])
    if prompt:
        ckv_b = c_kv.astype(BF16)
        kn_ref[...] = _dot(ckv_b, wuk_ref[...]).astype(BF16)
        vt_ref[...] = _dot_nt(wuvt_ref[...], ckv_b).astype(BF16)
        krb_ref[...] = k_r.astype(BF16)


def _mla_proj(x, g, wd_ext, qg, kvg, wuq_ext, cos, sin, extra, *, heads, seq_len, prompt):
    t, d = x.shape
    tm = min(PROMPT_TILE, seq_len) if prompt else t
    tps = seq_len // tm if prompt else 1
    hn = heads * NOPE_DIM
    in_specs = [pl.BlockSpec((tm, d), lambda i: (i, 0)),
                _const_spec((1, d)), _const_spec(wd_ext.shape), _const_spec((1, Q_RANK)),
                _const_spec((1, KV_RANK)), _const_spec(wuq_ext.shape),
                pl.BlockSpec((tm, LANES), lambda i: (i % tps, 0)),
                pl.BlockSpec((tm, LANES), lambda i: (i % tps, 0))]
    in_specs += [_const_spec(w.shape) for w in extra]
    row = lambda w: pl.BlockSpec((tm, w), lambda i: (i, 0))
    if prompt:
        out_specs = [row(2 * hn), row(KV_RANK), row(LANES), row(hn),
                     pl.BlockSpec((hn, tm), lambda i: (0, i)), row(LANES)]
        out_shape = [jax.ShapeDtypeStruct((t, 2 * hn), BF16), jax.ShapeDtypeStruct((t, KV_RANK), F32),
                     jax.ShapeDtypeStruct((t, LANES), F32), jax.ShapeDtypeStruct((t, hn), BF16),
                     jax.ShapeDtypeStruct((hn, t), BF16), jax.ShapeDtypeStruct((t, LANES), BF16)]
    else:
        head_major = lambda w: pl.BlockSpec((heads, tm, w), lambda i: (0, i, 0))
        out_specs = [head_major(KV_RANK), head_major(LANES), row(KV_RANK), row(LANES)]
        out_shape = [jax.ShapeDtypeStruct((heads, t, KV_RANK), F32), jax.ShapeDtypeStruct((heads, t, LANES), F32),
                     jax.ShapeDtypeStruct((t, KV_RANK), F32), jax.ShapeDtypeStruct((t, LANES), F32)]
    return pl.pallas_call(
        functools.partial(_mla_proj_kernel, heads=heads, prompt=prompt),
        grid=(t // tm,),
        in_specs=in_specs,
        out_specs=out_specs,
        out_shape=out_shape,
        compiler_params=_params("arbitrary"),
        name="mla_proj_prompt" if prompt else "mla_proj_sample",
    )(x, g.reshape(1, d), wd_ext, qg.reshape(1, Q_RANK), kvg.reshape(1, KV_RANK), wuq_ext, cos, sin, *extra)


def _attn_prompt_kernel(qn_ref, qr_ref, kn_ref, kr_ref, vt_ref, o_ref,
                        sa_scr, sb_scr, m_scr, l_scr, acc_scr, *, t, scale):
    i = pl.program_id(2)
    hps = sa_scr.shape[0]
    head = lambda g: slice(g * NOPE_DIM, (g + 1) * NOPE_DIM)
    qs = [jnp.concatenate([qn_ref[:, head(g)], qr_ref[:, head(g)]], axis=1) for g in range(hps)]
    c = scale * LOG2_E
    m_scr[...] = jnp.full(m_scr.shape, -jnp.inf, F32)
    l_scr[...] = jnp.zeros(l_scr.shape, F32)
    acc_scr[...] = jnp.zeros(acc_scr.shape, F32)

    def scores(j, dst):
        ks = pl.multiple_of(j * t, t)
        kr = kr_ref[pl.ds(ks, t), :]
        for g in range(hps):
            k = jnp.concatenate([kn_ref[pl.ds(ks, t), head(g)], kr], axis=1)
            dst[g] = _dot_nt(k, qs[g])

    def update(src, j, diagonal):
        ks = pl.multiple_of(j * t, t)
        for g in range(hps):
            s = src[g]
            if diagonal:
                kpos = lax.broadcasted_iota(jnp.int32, (t, t), 0)
                qpos = lax.broadcasted_iota(jnp.int32, (t, t), 1)
                s = jnp.where(kpos <= qpos, s, -jnp.inf)
            m = m_scr[g]
            m_new = jnp.maximum(m, jnp.max(s, axis=0, keepdims=True))
            alpha = jnp.exp2((m - m_new) * c)
            p = jnp.exp2((s - m_new) * c)
            l_scr[g] = alpha * l_scr[g] + jnp.sum(p, axis=0, keepdims=True)
            acc_scr[g] = alpha * acc_scr[g] + _dot(vt_ref[head(g), pl.ds(ks, t)], p.astype(BF16))
            m_scr[g] = m_new

    scores(0, sa_scr)

    def pair(jj, _):
        scores(2 * jj + 1, sb_scr)
        update(sa_scr, 2 * jj, False)
        scores(2 * jj + 2, sa_scr)
        update(sb_scr, 2 * jj + 1, False)
        return 0
    lax.fori_loop(0, i // 2, pair, 0)

    @pl.when(i % 2 == 0)
    def _():
        update(sa_scr, i, True)

    @pl.when(i % 2 == 1)
    def _():
        scores(i, sb_scr)
        update(sa_scr, i - 1, False)
        update(sb_scr, i, True)

    for g in range(hps):
        o_ref[:, head(g)] = (acc_scr[g] * (1.0 / l_scr[g])).T.astype(o_ref.dtype)


def _attn_prompt(q, kn, krb, vt, *, heads, seq_len, scale):
    tokens = q.shape[0]
    bsz = tokens // seq_len
    t = min(ATTN_TILE, seq_len)
    nq = seq_len // t
    hps = ATTN_HEADS_PER_STEP
    hg = heads // hps
    w = hps * NOPE_DIM
    return pl.pallas_call(
        functools.partial(_attn_prompt_kernel, t=t, scale=scale),
        grid=(bsz, hg, nq),
        in_specs=[pl.BlockSpec((t, w), lambda b, h, i: (b * nq + i, h)),
                  pl.BlockSpec((t, w), lambda b, h, i: (b * nq + i, hg + h)),
                  pl.BlockSpec((seq_len, w), lambda b, h, i: (b, h)),
                  pl.BlockSpec((seq_len, LANES), lambda b, h, i: (b, 0)),
                  pl.BlockSpec((w, seq_len), lambda b, h, i: (h, b))],
        out_specs=pl.BlockSpec((t, w), lambda b, h, i: (b * nq + i, h)),
        out_shape=jax.ShapeDtypeStruct((tokens, heads * NOPE_DIM), BF16),
        scratch_shapes=[pltpu.VMEM((hps, t, t), F32), pltpu.VMEM((hps, t, t), F32), pltpu.VMEM((hps, 1, t), F32),
                        pltpu.VMEM((hps, 1, t), F32), pltpu.VMEM((hps, NOPE_DIM, t), F32)],
        compiler_params=_params("arbitrary", "arbitrary", "arbitrary"),
        name="mla_attn_prompt",
    )(q, q, kn, krb, vt)


def _attn_sample_kernel(pt_ref, ql_ref, qr_ref, cn_ref, kn_ref, lat_hbm, ropet_hbm, o_ref,
                        lat_buf, rope_buf, s_scr, sem, *, layer, n_pages, page, ppc, scale):
    b = pl.program_id(0)
    nb = pl.num_programs(0)
    slot = b % 2
    heads, ts, _ = ql_ref.shape
    tk = ppc * page
    n_chunks = n_pages // ppc

    def start_fetch(seq, dst_slot):
        for p in range(n_pages):
            pg = pt_ref[seq, p]
            pltpu.make_async_copy(lat_hbm.at[layer, pg], lat_buf.at[dst_slot, p * page:(p + 1) * page, :],
                                  sem.at[0, dst_slot]).start()
            pltpu.make_async_copy(ropet_hbm.at[layer, pg], rope_buf.at[dst_slot, p], sem.at[1, dst_slot]).start()

    def wait_fetch(dst_slot):
        pltpu.make_async_copy(lat_buf.at[dst_slot], lat_buf.at[dst_slot], sem.at[0, dst_slot]).wait()
        pltpu.make_async_copy(rope_buf.at[dst_slot], rope_buf.at[dst_slot], sem.at[1, dst_slot]).wait()

    @pl.when(b == 0)
    def _():
        start_fetch(0, 0)

    wait_fetch(slot)
    start_fetch((b + 1) % nb, 1 - slot)

    ql = ql_ref[...].reshape(heads * ts, KV_RANK).astype(BF16)
    qr = qr_ref[...].reshape(heads * ts, LANES)[:, :ROPE_DIM].astype(BF16)

    def lat_chunk(c):
        return lat_buf[slot, c * tk:(c + 1) * tk, :]

    for c in range(n_chunks):
        ropt = jnp.concatenate([rope_buf[slot, c * ppc + k] for k in range(ppc)], axis=1).astype(BF16)
        lat_t = lat_chunk(c).T.astype(BF16)
        s_scr[:, c * tk:(c + 1) * tk] = (_dot(ql, lat_t) + _dot(qr, ropt)) * scale

    pad = 2 * SUBLANES - ts
    cn = jnp.concatenate([cn_ref[...], jnp.zeros((pad, KV_RANK), F32)], axis=0).astype(BF16)
    kn = jnp.concatenate([kn_ref[...][:, :ROPE_DIM], jnp.zeros((pad, ROPE_DIM), F32)], axis=0).astype(BF16)
    s_new = (_dot_nt(ql, cn) + _dot_nt(qr, kn)) * scale
    t_q = lax.broadcasted_iota(jnp.int32, s_new.shape, 0) % ts
    t_k = lax.broadcasted_iota(jnp.int32, s_new.shape, 1)
    s_new = jnp.where(t_k <= t_q, s_new, -jnp.inf)

    m = jnp.maximum(jnp.max(s_scr[...], axis=-1, keepdims=True), jnp.max(s_new, axis=-1, keepdims=True))
    p_new = jnp.exp(s_new - m)
    l = jnp.sum(p_new, axis=-1, keepdims=True)
    acc = _dot(p_new.astype(BF16), cn)
    for c in range(n_chunks):
        p = jnp.exp(s_scr[:, c * tk:(c + 1) * tk] - m)
        l = l + jnp.sum(p, axis=-1, keepdims=True)
        acc = acc + _dot(p.astype(BF16), lat_chunk(c).astype(BF16))
    o_ref[...] = (acc * (1.0 / l)).reshape(heads, ts, KV_RANK)

    @pl.when(b == nb - 1)
    def _():
        wait_fetch(1 - slot)


def _attn_sample(page_table, ql, qr, ckv, kr, cache_lat, cache_rope_t, *, layer, ts, scale):
    heads, t, _ = ql.shape
    n_seq, n_pages = page_table.shape
    page = cache_lat.shape[2]
    n_past = n_pages * page
    ppc = min(PAST_CHUNK_PAGES, n_pages)
    grid_spec = pltpu.PrefetchScalarGridSpec(
        num_scalar_prefetch=1,
        grid=(n_seq,),
        in_specs=[pl.BlockSpec((heads, ts, KV_RANK), lambda b, pt: (0, b, 0)),
                  pl.BlockSpec((heads, ts, LANES), lambda b, pt: (0, b, 0)),
                  pl.BlockSpec((ts, KV_RANK), lambda b, pt: (b, 0)),
                  pl.BlockSpec((ts, LANES), lambda b, pt: (b, 0)),
                  pl.BlockSpec(memory_space=pl.ANY),
                  pl.BlockSpec(memory_space=pl.ANY)],
        out_specs=pl.BlockSpec((heads, ts, KV_RANK), lambda b, pt: (0, b, 0)),
        scratch_shapes=[pltpu.VMEM((2, n_past, KV_RANK), F32),
                        pltpu.VMEM((2, n_pages, ROPE_DIM, page), F32),
                        pltpu.VMEM((heads * ts, n_past), F32),
                        pltpu.SemaphoreType.DMA((2, 2))],
    )
    return pl.pallas_call(
        functools.partial(_attn_sample_kernel, layer=layer, n_pages=n_pages, page=page, ppc=ppc, scale=scale),
        grid_spec=grid_spec,
        out_shape=jax.ShapeDtypeStruct((heads, t, KV_RANK), F32),
        compiler_params=_params("arbitrary"),
        name="mla_attn_sample",
    )(page_table, ql, qr, ckv, kr, cache_lat, cache_rope_t)


def _mla_sample_out_kernel(ol_ref, wuv_ref, wo_ref, x_ref, o_ref, os_scr):
    heads = ol_ref.shape[0]
    vd = wuv_ref.shape[2]
    for h in range(heads):
        os_scr[:, h * vd:(h + 1) * vd] = _dot(ol_ref[h].astype(BF16), wuv_ref[h]).astype(BF16)
    o_ref[...] = x_ref[...] + _dot(os_scr[...], wo_ref[...])


def _mla_sample_out(ol, wuv_h, wo, x):
    t, d = x.shape
    heads, _, vd = wuv_h.shape
    return pl.pallas_call(
        _mla_sample_out_kernel,
        grid=(1,),
        in_specs=[_const_spec(ol.shape), _const_spec(wuv_h.shape), _const_spec(wo.shape), _const_spec(x.shape)],
        out_specs=pl.BlockSpec((t, d), lambda i: (0, 0)),
        out_shape=jax.ShapeDtypeStruct((t, d), F32),
        scratch_shapes=[pltpu.VMEM((t, heads * vd), BF16)],
        compiler_params=_params("arbitrary"),
        name="mla_out_sample",
    )(ol, wuv_h, wo, x)


def _matmul_residual_kernel(a_ref, w_ref, x_ref, o_ref):
    o_ref[...] = x_ref[...] + _dot(a_ref[...].astype(BF16), w_ref[...])


def _matmul_residual(a, w, x, *, tm, layer=None):
    t, d = x.shape
    k = a.shape[1]
    return pl.pallas_call(
        _matmul_residual_kernel,
        grid=(t // tm,),
        in_specs=[pl.BlockSpec((tm, k), lambda i: (i, 0)), _weight_spec(w, layer),
                  pl.BlockSpec((tm, d), lambda i: (i, 0))],
        out_specs=pl.BlockSpec((tm, d), lambda i: (i, 0)),
        out_shape=jax.ShapeDtypeStruct((t, d), F32),
        compiler_params=_params("arbitrary"),
        name="matmul_residual",
    )(a, w, x)


def _mem_kv_kernel(mem_ref, g_ref, wk_ref, wv_ref, k_ref, v_ref, mn_scr):
    @pl.when(pl.program_id(1) == 0)
    def _():
        mn_scr[...] = _rms(mem_ref[...], g_ref[...]).astype(BF16)
    mn = mn_scr[...]
    k_ref[...] = _dot(mn, wk_ref[...])
    v_ref[...] = _dot(mn, wv_ref[...])


def _mem_kv(mem, g, wkv, *, tn=512):
    m, d = mem.shape
    depth = wkv.shape[0]
    e = wkv.shape[2] // 2
    nk = e // tn
    return pl.pallas_call(
        _mem_kv_kernel,
        grid=(depth, nk),
        in_specs=[_const_spec((m, d)),
                  pl.BlockSpec((None, 1, d), lambda l, n: (l, 0, 0)),
                  pl.BlockSpec((None, d, tn), lambda l, n: (l, 0, n)),
                  pl.BlockSpec((None, d, tn), lambda l, n: (l, 0, nk + n))],
        out_specs=[pl.BlockSpec((None, m, tn), lambda l, n: (l, 0, n)),
                   pl.BlockSpec((None, m, tn), lambda l, n: (l, 0, n))],
        out_shape=[jax.ShapeDtypeStruct((depth, m, e), F32), jax.ShapeDtypeStruct((depth, m, e), F32)],
        scratch_shapes=[pltpu.VMEM((m, d), BF16)],
        compiler_params=_params("arbitrary", "arbitrary"),
        name="mem_kv",
    )(mem, g, wkv, wkv)


def _xattn_heads(q, k_of, v_of, heads, scale):
    e = q.shape[1] // heads
    outs = []
    for h in range(heads):
        s = _dot_nt(q[:, h * e:(h + 1) * e].astype(BF16), k_of(h)) * scale
        p = jnp.exp(s - jnp.max(s, axis=-1, keepdims=True))
        p = p * (1.0 / jnp.sum(p, axis=-1, keepdims=True))
        outs.append(_dot(p.astype(BF16), v_of(h)))
    return outs


def _xattn_prompt_kernel(x_ref, g_ref, wq_ref, mk_ref, mv_ref, wo_ref, o_ref, o_scr, *, heads, scale):
    x = x_ref[...]
    q = _dot(_rms(x, g_ref[...]).astype(BF16), wq_ref[...])
    e = q.shape[1] // heads
    outs = _xattn_heads(q, lambda h: mk_ref[:, h * e:(h + 1) * e].astype(BF16),
                        lambda h: mv_ref[:, h * e:(h + 1) * e].astype(BF16), heads, scale)
    for h in range(heads):
        o_scr[:, h * e:(h + 1) * e] = outs[h].astype(BF16)
    o_ref[...] = x + _dot(o_scr[...], wo_ref[...])


def _xattn_prompt(x, g, wq, mk_all, mv_all, wo, *, layer, heads, seq_len, n_mem, scale):
    t, d = x.shape
    tm = min(PROMPT_TILE, seq_len)
    tps = seq_len // tm
    return pl.pallas_call(
        functools.partial(_xattn_prompt_kernel, heads=heads, scale=scale),
        grid=(t // tm,),
        in_specs=[pl.BlockSpec((tm, d), lambda i: (i, 0)),
                  _weight_spec(g, layer), _weight_spec(wq, layer),
                  pl.BlockSpec((None, n_mem, d), lambda i: (layer, i // tps, 0)),
                  pl.BlockSpec((None, n_mem, d), lambda i: (layer, i // tps, 0)),
                  _weight_spec(wo, layer)],
        out_specs=pl.BlockSpec((tm, d), lambda i: (i, 0)),
        out_shape=jax.ShapeDtypeStruct((t, d), F32),
        scratch_shapes=[pltpu.VMEM((tm, d), BF16)],
        compiler_params=_params("arbitrary"),
        name="xattn_prompt",
    )(x, g, wq, mk_all, mv_all, wo)


def _norm_matmul_kernel(x_ref, g_ref, w_ref, o_ref):
    o_ref[...] = _dot(_rms(x_ref[...], g_ref[...]).astype(BF16), w_ref[...])


def _norm_matmul(x, g, w, *, layer):
    t, d = x.shape
    n = w.shape[-1]
    return pl.pallas_call(
        _norm_matmul_kernel,
        grid=(1,),
        in_specs=[_const_spec((t, d)), _weight_spec(g, layer), _weight_spec(w, layer)],
        out_specs=pl.BlockSpec((t, n), lambda i: (0, 0)),
        out_shape=jax.ShapeDtypeStruct((t, n), F32),
        compiler_params=_params("arbitrary"),
        name="norm_matmul",
    )(x, g, w)


def _xattn_sample_kernel(q_ref, mk_ref, mv_ref, o_ref, *, ts, scale):
    nseq, _, heads, e = mk_ref.shape
    for s_i in range(nseq):
        q = q_ref[s_i * ts:(s_i + 1) * ts, :]
        q_hm = jnp.concatenate([q[:, h * e:(h + 1) * e] for h in range(heads)], axis=0).astype(BF16)
        k_all = mk_ref[s_i].reshape(-1, e).astype(BF16)
        v_all = mv_ref[s_i].reshape(-1, e).astype(BF16)
        s = _dot_nt(q_hm, k_all) * scale
        key_head = lax.broadcasted_iota(jnp.int32, s.shape, 1) % heads
        row_head = lax.broadcasted_iota(jnp.int32, s.shape, 0) // ts
        s = jnp.where(key_head == row_head, s, -jnp.inf)
        p = jnp.exp(s - jnp.max(s, axis=-1, keepdims=True))
        p = p * (1.0 / jnp.sum(p, axis=-1, keepdims=True))
        o = _dot(p.astype(BF16), v_all)
        for h in range(heads):
            o_ref[s_i * ts:(s_i + 1) * ts, h * e:(h + 1) * e] = o[h * ts:(h + 1) * ts]


def _xattn_sample(q, mk_all, mv_all, *, layer, ts, scale, seqs_per_step=4):
    t, d = q.shape
    _, n_seq, n_mem, heads, e = mk_all.shape
    sp = min(seqs_per_step, n_seq)
    cache_spec = pl.BlockSpec((None, sp, n_mem, heads, e), lambda i: (layer, i, 0, 0, 0))
    return pl.pallas_call(
        functools.partial(_xattn_sample_kernel, ts=ts, scale=scale),
        grid=(n_seq // sp,),
        in_specs=[pl.BlockSpec((sp * ts, d), lambda i: (i, 0)), cache_spec, cache_spec],
        out_specs=pl.BlockSpec((sp * ts, d), lambda i: (i, 0)),
        out_shape=jax.ShapeDtypeStruct((t, d), F32),
        compiler_params=_params("arbitrary"),
        name="xattn_sample",
    )(q, mk_all, mv_all)


def _rope_tables(pos):
    half = ROPE_DIM // 2
    inv = 1.0 / (ROPE_THETA ** (jnp.arange(half, dtype=F32) * (2.0 / ROPE_DIM)))
    ang = pos.astype(F32)[:, None] * inv[None, :]
    reps = LANES // half
    return jnp.tile(jnp.cos(ang), (1, reps)), jnp.tile(jnp.sin(ang), (1, reps))


def _rot_cols(w):
    half = w.shape[-1] // 2
    return jnp.concatenate([-w[..., half:], w[..., :half]], axis=-1)


def _pad_lanes(w):
    return jnp.pad(w, [(0, 0)] * (w.ndim - 1) + [(0, LANES - w.shape[-1])])


def _state_to_prefix8(state):
    return jnp.pad(state, ((0, 0), (SUBLANES - state.shape[1], 0), (0, 0)))


def kernel(x_prompt, x_sample, state_conv, cache_kv_latent, cache_k_rope, state_ffn_conv, cache_mem_k, cache_mem_v, page_table, mem_prompt, conv_norm_g, conv_w_in, conv_w, conv_w_out, mla_norm_g, mla_w_down, mla_q_norm_g, mla_kv_norm_g, mla_w_uq, mla_w_uk, mla_w_uv, mla_w_o, xa_norm_g, xa_mem_norm_g, xa_w_q, xa_w_kv, xa_w_o, ffn_norm_g, ffn_w_up, ffn_conv_w, ffn_w_down, final_norm_g):
    bp, sp, d = x_prompt.shape
    bs, ts, _ = x_sample.shape
    depth = ffn_w_up.shape[0]
    d_ff = ffn_w_down.shape[1]
    heads = mla_w_uq.shape[2]
    xa_heads, xa_dim = xa_w_q.shape[2], xa_w_q.shape[3]
    n_mem = mem_prompt.shape[1]
    n_pages, page = page_table.shape[1], cache_kv_latent.shape[2]
    score_scale = (NOPE_DIM + ROPE_DIM) ** -0.5
    xa_scale = xa_dim ** -0.5
    assert ts == SUBLANES and KV_RANK + Q_RANK + ROPE_DIM == mla_w_down.shape[2]

    xp = x_prompt.reshape(bp * sp, d)
    xs = x_sample.reshape(bs * ts, d)

    cos_p, sin_p = _rope_tables(jnp.arange(sp, dtype=jnp.int32))
    cos_s, sin_s = _rope_tables(n_pages * page + jnp.arange(ts, dtype=jnp.int32))
    cos_s, sin_s = jnp.tile(cos_s, (bs, 1)), jnp.tile(sin_s, (bs, 1))

    xa_e = xa_heads * xa_dim
    gain = lambda g: g.reshape(g.shape[0], 1, g.shape[1])
    conv_g, xa_g, ffn_g = gain(conv_norm_g), gain(xa_norm_g), gain(ffn_norm_g)
    w_in_all, w_out_all = conv_w_in.astype(BF16), conv_w_out.astype(BF16)
    wq_all = xa_w_q.reshape(depth, d, xa_e).astype(BF16)
    wo_x_all = xa_w_o.reshape(depth, xa_e, d).astype(BF16)
    w_up_all, w_dn_all = ffn_w_up.astype(BF16), ffn_w_down.astype(BF16)

    wkv_all = xa_w_kv.reshape(depth, d, 2 * xa_e).astype(BF16)
    mk_all, mv_all = _mem_kv(mem_prompt.reshape(bp * n_mem, d), gain(xa_mem_norm_g), wkv_all)
    cache_rope_t = jnp.swapaxes(cache_k_rope, 2, 3)

    conv_p, conv_s, lat_p, rop_p, lat_s, rop_s, ffn_p, ffn_s = [], [], [], [], [], [], [], []

    for i in range(depth):
        j = i // N_MIXERS
        if i % N_MIXERS == 0:
            xp, st = _conv_mixer(xp, None, conv_g, w_in_all, conv_w, w_out_all, layer=j, seq_len=sp)
            conv_p.append(st[:, SUBLANES - 2:, :])
            xs, st = _conv_mixer(xs, _state_to_prefix8(state_conv[j]), conv_g, w_in_all, conv_w, w_out_all,
                                 layer=j, seq_len=ts)
            conv_s.append(st[:, SUBLANES - 2:, :])
        else:
            wd = mla_w_down[j]
            w_r = wd[:, Q_RANK + KV_RANK:]
            wd_ext = jnp.concatenate([wd[:, :Q_RANK + KV_RANK], _pad_lanes(w_r), _pad_lanes(_rot_cols(w_r))],
                                     axis=1).astype(BF16)
            wuq = mla_w_uq[j]
            wuq_r = wuq[:, :, NOPE_DIM:]
            wuq_ext = jnp.concatenate([wuq[:, :, :NOPE_DIM].reshape(Q_RANK, heads * NOPE_DIM),
                                       _pad_lanes(wuq_r).reshape(Q_RANK, heads * LANES),
                                       _pad_lanes(_rot_cols(wuq_r)).reshape(Q_RANK, heads * LANES)],
                                      axis=1).astype(BF16)
            wuk = mla_w_uk[j].astype(BF16)
            wuv = mla_w_uv[j].astype(BF16)
            wo = mla_w_o[j].reshape(heads * NOPE_DIM, d).astype(BF16)

            q, ckv, kr, kn, vt, krb = _mla_proj(
                xp, mla_norm_g[j], wd_ext, mla_q_norm_g[j], mla_kv_norm_g[j], wuq_ext, cos_p, sin_p,
                [wuk.reshape(KV_RANK, heads * NOPE_DIM), wuv.reshape(KV_RANK, heads * NOPE_DIM).T],
                heads=heads, seq_len=sp, prompt=True)
            op = _attn_prompt(q, kn, krb, vt, heads=heads, seq_len=sp, scale=score_scale)
            xp = _matmul_residual(op, wo, xp, tm=min(PROMPT_TILE, sp))
            lat_p.append(ckv.reshape(bp, sp, KV_RANK))
            rop_p.append(kr[:, :ROPE_DIM].reshape(bp, sp, ROPE_DIM))

            ql, qr, ckv_s, kr_s = _mla_proj(
                xs, mla_norm_g[j], wd_ext, mla_q_norm_g[j], mla_kv_norm_g[j], wuq_ext, cos_s, sin_s,
                [wuk.transpose(1, 2, 0)], heads=heads, seq_len=ts, prompt=False)
            ol = _attn_sample(page_table, ql, qr, ckv_s, kr_s, cache_kv_latent, cache_rope_t,
                              layer=j, ts=ts, scale=score_scale)
            xs = _mla_sample_out(ol, wuv.transpose(1, 0, 2), wo, xs)
            lat_s.append(ckv_s.reshape(bs, ts, KV_RANK))
            rop_s.append(kr_s[:, :ROPE_DIM].reshape(bs, ts, ROPE_DIM))

        xp = _xattn_prompt(xp, xa_g, wq_all, mk_all, mv_all, wo_x_all, layer=i, heads=xa_heads, seq_len=sp,
                           n_mem=n_mem, scale=xa_scale)
        qs = _norm_matmul(xs, xa_g, wq_all, layer=i)
        os_ = _xattn_sample(qs, cache_mem_k, cache_mem_v, layer=i, ts=ts, scale=xa_scale)
        xs = _matmul_residual(os_, wo_x_all, xs, tm=bs * ts, layer=i)

        fg = final_norm_g if i == depth - 1 else None
        xp, st = _conv_ffn(xp, None, ffn_g, w_up_all, ffn_conv_w, w_dn_all, fg, layer=i, seq_len=sp)
        ffn_p.append(st[:, SUBLANES - 2:, :])
        xs, st = _conv_ffn(xs, _state_to_prefix8(state_ffn_conv[i]), ffn_g, w_up_all, ffn_conv_w, w_dn_all,
                           fg, layer=i, seq_len=ts)
        ffn_s.append(st[:, SUBLANES - 2:, :])

    mem_shape = (depth, bp, n_mem, xa_heads, xa_dim)
    return (xp.reshape(bp, sp, d), xs.reshape(bs, ts, d),
            jnp.stack(conv_p), jnp.stack(conv_s),
            jnp.stack(lat_p), jnp.stack(rop_p), jnp.stack(lat_s), jnp.stack(rop_s),
            jnp.stack(ffn_p), jnp.stack(ffn_s),
            mk_all.reshape(mem_shape), mv_all.reshape(mem_shape))
```

```python
import functools
import math

import jax
import jax.numpy as jnp
from jax import lax
from jax.experimental import pallas as pl
from jax.experimental.pallas import tpu as pltpu

F32 = jnp.float32
BF16 = jnp.bfloat16

RMS_EPS = 1e-6
ROPE_THETA = 10000.0
N_MIXERS = 2
NOPE_DIM = 128
ROPE_DIM = 64
Q_RANK = 384
KV_RANK = 256

SUBLANES = 8
BF16_SUBLANES = 16
LANES = 128
MXU_WIDTH = 256
VMEM_LIMIT_BYTES = 56 * 1024 * 1024

PROMPT_TILE = 512
CHUNK = MXU_WIDTH
ATTN_TILE = 512
ATTN_HEADS_PER_STEP = 4
PAST_SEQS_PER_STEP = 2
PAST_CHUNK_PAGES = 8
LOG2_E = math.log2(math.e)


def _dot(a, b):
    return jnp.dot(a, b, preferred_element_type=F32)


def _dot_nt(a, b):
    return lax.dot_general(a, b, (((1,), (1,)), ((), ())), preferred_element_type=F32)


def _rms(xf, g):
    y = xf * lax.rsqrt(jnp.mean(xf * xf, axis=-1, keepdims=True) + RMS_EPS)
    return y * g


def _params(*sem):
    return pltpu.CompilerParams(dimension_semantics=tuple(sem), vmem_limit_bytes=VMEM_LIMIT_BYTES)


def _const_spec(shape):
    nd = len(shape)
    return pl.BlockSpec(shape, lambda *_: (0,) * nd, pipeline_mode=pl.Buffered(1))


def _weight_spec(w, layer=None):
    if layer is None:
        return _const_spec(w.shape)
    nd = w.ndim - 1
    return pl.BlockSpec((None,) + w.shape[1:], lambda *_: (layer,) + (0,) * nd, pipeline_mode=pl.Buffered(1))


def _conv3(u3, prefix8, w, ext_ref):
    r = u3.shape[1]
    ext_ref[:, 0:SUBLANES, :] = prefix8
    ext_ref[:, SUBLANES:SUBLANES + r, :] = u3
    y = (ext_ref[:, SUBLANES - 2:SUBLANES - 2 + r, :] * w[0:1, :][None]
         + ext_ref[:, SUBLANES - 1:SUBLANES - 1 + r, :] * w[1:2, :][None]
         + u3 * w[2:3, :][None])
    return y, ext_ref[:, r:r + SUBLANES, :]


def _zero_at_sequence_start(carry_ref, tiles_per_seq):
    @pl.when(pl.program_id(0) % tiles_per_seq == 0)
    def _():
        carry_ref[...] = jnp.zeros(carry_ref.shape, carry_ref.dtype)


def _conv_mixer_kernel(*refs, groups, rows, tiles_per_seq, has_prefix):
    if has_prefix:
        x_ref, pre_ref, g_ref, win_ref, cw_ref, wout_ref, o_ref, st_ref, xn_scr, ext_scr, p_scr = refs
    else:
        x_ref, g_ref, win_ref, cw_ref, wout_ref, o_ref, st_ref, xn_scr, ext_scr, p_scr, carry_scr = refs
        _zero_at_sequence_start(carry_scr, tiles_per_seq)
    tm = groups * rows
    d = x_ref.shape[1]
    x = x_ref[...]
    xn_scr[...] = _rms(x, g_ref[...]).astype(BF16)
    for n in range(d // CHUNK):
        lo, hi = n * CHUNK, (n + 1) * CHUNK
        xn = xn_scr[...]
        b = _dot(xn, win_ref[:, lo:hi])
        c = _dot(xn, win_ref[:, d + lo:d + hi])
        h = _dot(xn, win_ref[:, 2 * d + lo:2 * d + hi])
        u3 = (c * h).reshape(groups, rows, CHUNK)
        if has_prefix:
            pre = pre_ref[:, :, lo:hi]
        else:
            pre = carry_scr[:, :, lo:hi]
        y3, new8 = _conv3(u3, pre, cw_ref[:, lo:hi], ext_scr)
        st_ref[:, :, lo:hi] = new8
        if not has_prefix:
            carry_scr[:, :, lo:hi] = new8
        p_scr[:, lo:hi] = (b * y3.reshape(tm, CHUNK)).astype(BF16)
    o_ref[...] = x + _dot(p_scr[...], wout_ref[...])


def _conv_mixer(x, prefix8, g, w_in, cw, w_out, *, layer, seq_len):
    t, d = x.shape
    has_prefix = prefix8 is not None
    if has_prefix:
        groups, rows, tm, tps, n_state = t // SUBLANES, SUBLANES, t, 1, t // SUBLANES
    else:
        tm = min(PROMPT_TILE, seq_len)
        groups, rows, tps, n_state = 1, tm, seq_len // tm, t // seq_len
    grid = (t // tm,)
    in_specs = [pl.BlockSpec((tm, d), lambda i: (i, 0))]
    args = [x]
    if has_prefix:
        in_specs.append(pl.BlockSpec((groups, SUBLANES, d), lambda i: (0, 0, 0)))
        args.append(prefix8)
    in_specs += [_weight_spec(w, layer) for w in (g, w_in, cw, w_out)]
    args += [g, w_in, cw, w_out]
    scratch = [pltpu.VMEM((tm, d), BF16),
               pltpu.VMEM((groups, SUBLANES + rows, CHUNK), F32),
               pltpu.VMEM((tm, d), BF16)]
    if not has_prefix:
        scratch.append(pltpu.VMEM((1, SUBLANES, d), F32))
    st_block = (groups, SUBLANES, d)
    return pl.pallas_call(
        functools.partial(_conv_mixer_kernel, groups=groups, rows=rows, tiles_per_seq=tps,
                          has_prefix=has_prefix),
        grid=grid,
        in_specs=in_specs,
        out_specs=[pl.BlockSpec((tm, d), lambda i: (i, 0)),
                   pl.BlockSpec(st_block, lambda i: (i // tps, 0, 0))],
        out_shape=[jax.ShapeDtypeStruct((t, d), F32),
                   jax.ShapeDtypeStruct((n_state, SUBLANES, d), F32)],
        scratch_shapes=scratch,
        compiler_params=_params("arbitrary"),
        name="conv_mixer_sample" if has_prefix else "conv_mixer_prompt",
    )(*args)


def _ffn_kernel(*refs, groups, rows, tiles_per_seq, has_prefix, final):
    refs = list(refs)
    x_ref = refs.pop(0)
    pre_ref = refs.pop(0) if has_prefix else None
    g_ref, wup_ref, cw_ref, wd_ref = refs[:4]
    refs = refs[4:]
    gf_ref = refs.pop(0) if final else None
    o_ref, st_ref, xn_scr, ext_scr, h_scr = refs[:5]
    carry_scr = None if has_prefix else refs[5]
    if not has_prefix:
        _zero_at_sequence_start(carry_scr, tiles_per_seq)
    tm = groups * rows
    f = wd_ref.shape[0]
    x = x_ref[...]
    xn_scr[...] = _rms(x, g_ref[...]).astype(BF16)
    for n in range(f // CHUNK):
        lo, hi = n * CHUNK, (n + 1) * CHUNK
        xn = xn_scr[...]
        ug = _dot(xn, wup_ref[:, lo:hi])
        uv = _dot(xn, wup_ref[:, f + lo:f + hi])
        if has_prefix:
            pre = pre_ref[:, :, lo:hi]
        else:
            pre = carry_scr[:, :, lo:hi]
        gc3, new8 = _conv3(ug.reshape(groups, rows, CHUNK), pre, cw_ref[:, lo:hi], ext_scr)
        st_ref[:, :, lo:hi] = new8
        if not has_prefix:
            carry_scr[:, :, lo:hi] = new8
        gc = gc3.reshape(tm, CHUNK)
        h_scr[:, lo:hi] = ((gc * jax.nn.sigmoid(gc)) * uv).astype(BF16)
    res = x + _dot(h_scr[...], wd_ref[...])
    o_ref[...] = _rms(res, gf_ref[...]) if final else res


def _conv_ffn(x, prefix8, g, w_up, cw, wd, final_g, *, layer, seq_len):
    t, d = x.shape
    f = wd.shape[1]
    has_prefix = prefix8 is not None
    final = final_g is not None
    if has_prefix:
        groups, rows, tm, tps, n_state = t // SUBLANES, SUBLANES, t, 1, t // SUBLANES
    else:
        tm = min(PROMPT_TILE, seq_len)
        groups, rows, tps, n_state = 1, tm, seq_len // tm, t // seq_len
    in_specs = [pl.BlockSpec((tm, d), lambda i: (i, 0))]
    args = [x]
    if has_prefix:
        in_specs.append(pl.BlockSpec((groups, SUBLANES, f), lambda i: (0, 0, 0)))
        args.append(prefix8)
    in_specs += [_weight_spec(w, layer) for w in (g, w_up, cw, wd)]
    args += [g, w_up, cw, wd]
    if final:
        in_specs.append(_const_spec((1, d)))
        args.append(final_g.reshape(1, d))
    scratch = [pltpu.VMEM((tm, d), BF16),
               pltpu.VMEM((groups, SUBLANES + rows, CHUNK), F32),
               pltpu.VMEM((tm, f), BF16)]
    if not has_prefix:
        scratch.append(pltpu.VMEM((1, SUBLANES, f), F32))
    return pl.pallas_call(
        functools.partial(_ffn_kernel, groups=groups, rows=rows, tiles_per_seq=tps,
                          has_prefix=has_prefix, final=final),
        grid=(t // tm,),
        in_specs=in_specs,
        out_specs=[pl.BlockSpec((tm, d), lambda i: (i, 0)),
                   pl.BlockSpec((groups, SUBLANES, f), lambda i: (i // tps, 0, 0))],
        out_shape=[jax.ShapeDtypeStruct((t, d), F32),
                   jax.ShapeDtypeStruct((n_state, SUBLANES, f), F32)],
        scratch_shapes=scratch,
        compiler_params=_params("arbitrary"),
        name="conv_ffn_sample" if has_prefix else "conv_ffn_prompt",
    )(*args)


def _mla_proj_kernel(*refs, heads, prompt, q_scale):
    if prompt:
        (x_ref, g_ref, wd_ref, qg_ref, kvg_ref, wuq_ref, cos_ref, sin_ref, wuk_ref, wuvt_ref,
         q_ref, ckv_ref, kr_ref, kn_ref, vt_ref, krb_ref) = refs
    else:
        (x_ref, g_ref, wd_ref, qg_ref, kvg_ref, wuq_ref, cos_ref, sin_ref, wukt_ref,
         ql_ref, qr_ref, ckv_ref, kr_ref) = refs
    hn = heads * NOPE_DIM
    cos = cos_ref[...]
    sin = sin_ref[...]
    xn = _rms(x_ref[...], g_ref[...]).astype(BF16)
    dn = _dot(xn, wd_ref[...])
    c_q = _rms(dn[:, :Q_RANK], qg_ref[...]).astype(BF16)
    c_kv = _rms(dn[:, Q_RANK:Q_RANK + KV_RANK], kvg_ref[...])
    ckv_ref[...] = c_kv
    r0 = Q_RANK + KV_RANK
    k_r = dn[:, r0:r0 + LANES] * cos + dn[:, r0 + LANES:r0 + 2 * LANES] * sin
    kr_ref[...] = k_r
    q = _dot(c_q, wuq_ref[...])
    if prompt:
        q_ref[:, :hn] = (q[:, :hn] * q_scale).astype(BF16)
    for h in range(heads):
        lo = hn + h * LANES
        qr_h = q[:, lo:lo + LANES] * cos + q[:, lo + heads * LANES:lo + (heads + 1) * LANES] * sin
        if prompt:
            q_ref[:, lo:lo + LANES] = (qr_h * q_scale).astype(BF16)
        else:
            qr_ref[h] = qr_h
            qn_h = q[:, h * NOPE_DIM:(h + 1) * NOPE_DIM].astype(BF16)
            ql_ref[h] = _dot(qn_h, wukt_ref[h])
    if prompt:
        ckv_b = c_kv.astype(BF16)
        kn_ref[...] = _dot(ckv_b, wuk_ref[...]).astype(BF16)
        vt_ref[...] = _dot_nt(wuvt_ref[...], ckv_b).astype(BF16)
        krb_ref[...] = k_r.astype(BF16)


def _mla_proj(x, g, wd_ext, qg, kvg, wuq_ext, cos, sin, extra, *, heads, seq_len, prompt, q_scale=None):
    t, d = x.shape
    tm = min(PROMPT_TILE, seq_len) if prompt else t
    tps = seq_len // tm if prompt else 1
    hn = heads * NOPE_DIM
    in_specs = [pl.BlockSpec((tm, d), lambda i: (i, 0)),
                _const_spec((1, d)), _const_spec(wd_ext.shape), _const_spec((1, Q_RANK)),
                _const_spec((1, KV_RANK)), _const_spec(wuq_ext.shape),
                pl.BlockSpec((tm, LANES), lambda i: (i % tps, 0)),
                pl.BlockSpec((tm, LANES), lambda i: (i % tps, 0))]
    in_specs += [_const_spec(w.shape) for w in extra]
    row = lambda w: pl.BlockSpec((tm, w), lambda i: (i, 0))
    if prompt:
        out_specs = [row(2 * hn), row(KV_RANK), row(LANES), row(hn),
                     pl.BlockSpec((hn, tm), lambda i: (0, i)), row(LANES)]
        out_shape = [jax.ShapeDtypeStruct((t, 2 * hn), BF16), jax.ShapeDtypeStruct((t, KV_RANK), F32),
                     jax.ShapeDtypeStruct((t, LANES), F32), jax.ShapeDtypeStruct((t, hn), BF16),
                     jax.ShapeDtypeStruct((hn, t), BF16), jax.ShapeDtypeStruct((t, LANES), BF16)]
    else:
        head_major = lambda w: pl.BlockSpec((heads, tm, w), lambda i: (0, i, 0))
        out_specs = [head_major(KV_RANK), head_major(LANES), row(KV_RANK), row(LANES)]
        out_shape = [jax.ShapeDtypeStruct((heads, t, KV_RANK), F32), jax.ShapeDtypeStruct((heads, t, LANES), F32),
                     jax.ShapeDtypeStruct((t, KV_RANK), F32), jax.ShapeDtypeStruct((t, LANES), F32)]
    return pl.pallas_call(
        functools.partial(_mla_proj_kernel, heads=heads, prompt=prompt, q_scale=q_scale),
        grid=(t // tm,),
        in_specs=in_specs,
        out_specs=out_specs,
        out_shape=out_shape,
        compiler_params=_params("arbitrary"),
        name="mla_proj_prompt" if prompt else "mla_proj_sample",
    )(x, g.reshape(1, d), wd_ext, qg.reshape(1, Q_RANK), kvg.reshape(1, KV_RANK), wuq_ext, cos, sin, *extra)


def _attn_prompt_kernel(qn_ref, qr_ref, kn_ref, kr_ref, vt_ref, o_ref,
                        sa_scr, sb_scr, m_scr, acc_scr, *, t):
    i = pl.program_id(2)
    hps = sa_scr.shape[0]
    head = lambda g: slice(g * NOPE_DIM, (g + 1) * NOPE_DIM)
    qs = [jnp.concatenate([qn_ref[:, head(g)], qr_ref[:, head(g)]], axis=1) for g in range(hps)]
    m_scr[...] = jnp.full(m_scr.shape, -jnp.inf, F32)
    acc_scr[...] = jnp.zeros(acc_scr.shape, F32)
    ones = jnp.ones((BF16_SUBLANES, t), BF16)

    def scores(j, dst):
        ks = pl.multiple_of(j * t, t)
        kr = kr_ref[pl.ds(ks, t), :]
        for g in range(hps):
            k = jnp.concatenate([kn_ref[pl.ds(ks, t), head(g)], kr], axis=1)
            dst[g] = _dot_nt(k, qs[g])

    def update(src, j, diagonal):
        ks = pl.multiple_of(j * t, t)
        for g in range(hps):
            s = src[g]
            if diagonal:
                kpos = lax.broadcasted_iota(jnp.int32, (t, t), 0)
                qpos = lax.broadcasted_iota(jnp.int32, (t, t), 1)
                s = jnp.where(kpos <= qpos, s, -jnp.inf)
            m = m_scr[g]
            m_new = jnp.maximum(m, jnp.max(s, axis=0, keepdims=True))
            alpha = jnp.exp2(m - m_new)
            p = jnp.exp2(s - m_new).astype(BF16)
            v1 = jnp.concatenate([vt_ref[head(g), pl.ds(ks, t)], ones], axis=0)
            acc_scr[g] = alpha * acc_scr[g] + _dot(v1, p)
            m_scr[g] = m_new

    scores(0, sa_scr)

    def pair(jj, _):
        scores(2 * jj + 1, sb_scr)
        update(sa_scr, 2 * jj, False)
        scores(2 * jj + 2, sa_scr)
        update(sb_scr, 2 * jj + 1, False)
        return 0
    lax.fori_loop(0, i // 2, pair, 0)

    @pl.when(i % 2 == 0)
    def _():
        update(sa_scr, i, True)

    @pl.when(i % 2 == 1)
    def _():
        scores(i, sb_scr)
        update(sa_scr, i - 1, False)
        update(sb_scr, i, True)

    for g in range(hps):
        l = acc_scr[g, NOPE_DIM:NOPE_DIM + 1, :]
        o_ref[:, head(g)] = (acc_scr[g, :NOPE_DIM, :] * (1.0 / l)).T.astype(o_ref.dtype)


def _attn_prompt(q, kn, krb, vt, *, heads, seq_len):
    tokens = q.shape[0]
    bsz = tokens // seq_len
    t = min(ATTN_TILE, seq_len)
    nq = seq_len // t
    hps = ATTN_HEADS_PER_STEP
    hg = heads // hps
    w = hps * NOPE_DIM
    return pl.pallas_call(
        functools.partial(_attn_prompt_kernel, t=t),
        grid=(bsz, hg, nq),
        in_specs=[pl.BlockSpec((t, w), lambda b, h, i: (b * nq + i, h)),
                  pl.BlockSpec((t, w), lambda b, h, i: (b * nq + i, hg + h)),
                  pl.BlockSpec((seq_len, w), lambda b, h, i: (b, h)),
                  pl.BlockSpec((seq_len, LANES), lambda b, h, i: (b, 0)),
                  pl.BlockSpec((w, seq_len), lambda b, h, i: (h, b))],
        out_specs=pl.BlockSpec((t, w), lambda b, h, i: (b * nq + i, h)),
        out_shape=jax.ShapeDtypeStruct((tokens, heads * NOPE_DIM), BF16),
        scratch_shapes=[pltpu.VMEM((hps, t, t), F32), pltpu.VMEM((hps, t, t), F32), pltpu.VMEM((hps, 1, t), F32),
                        pltpu.VMEM((hps, NOPE_DIM + BF16_SUBLANES, t), F32)],
        compiler_params=_params("arbitrary", "arbitrary", "arbitrary"),
        name="mla_attn_prompt",
    )(q, q, kn, krb, vt)


def _attn_sample_kernel(pt_ref, ql_ref, qr_ref, cn_ref, kn_ref, lat_hbm, ropet_hbm, o_ref,
                        lat_buf, rope_buf, s_scr, sem, *, layer, n_pages, page, ppc, ts, scale):
    b = pl.program_id(0)
    nb = pl.num_programs(0)
    slot = b % 2
    heads = ql_ref.shape[0]
    spp = s_scr.shape[0]
    tk = ppc * page
    n_chunks = n_pages // ppc
    seq_rows = lambda k: slice(k * ts, (k + 1) * ts)

    def start_fetch(group, dst_slot):
        for k in range(spp):
            for p in range(n_pages):
                pg = pt_ref[group * spp + k, p]
                pltpu.make_async_copy(lat_hbm.at[layer, pg], lat_buf.at[dst_slot, k, p * page:(p + 1) * page, :],
                                      sem.at[0, dst_slot]).start()
                pltpu.make_async_copy(ropet_hbm.at[layer, pg], rope_buf.at[dst_slot, k, p],
                                      sem.at[1, dst_slot]).start()

    def wait_fetch(dst_slot):
        pltpu.make_async_copy(lat_buf.at[dst_slot], lat_buf.at[dst_slot], sem.at[0, dst_slot]).wait()
        pltpu.make_async_copy(rope_buf.at[dst_slot], rope_buf.at[dst_slot], sem.at[1, dst_slot]).wait()

    @pl.when(b == 0)
    def _():
        start_fetch(0, 0)

    wait_fetch(slot)
    start_fetch((b + 1) % nb, 1 - slot)

    qls = [ql_ref[:, seq_rows(k), :].reshape(heads * ts, KV_RANK).astype(BF16) for k in range(spp)]
    qrs = [qr_ref[:, seq_rows(k), :].reshape(heads * ts, LANES)[:, :ROPE_DIM].astype(BF16) for k in range(spp)]

    def lat_chunk(k, c):
        return lat_buf[slot, k, c * tk:(c + 1) * tk, :]

    for c in range(n_chunks):
        for k in range(spp):
            ropt = jnp.concatenate([rope_buf[slot, k, c * ppc + j] for j in range(ppc)], axis=1).astype(BF16)
            lat_t = lat_chunk(k, c).T.astype(BF16)
            s_scr[k, :, c * tk:(c + 1) * tk] = (_dot(qls[k], lat_t) + _dot(qrs[k], ropt)) * scale

    pad = BF16_SUBLANES - ts
    ms, ls, accs = [], [], []
    for k in range(spp):
        cn = jnp.concatenate([cn_ref[seq_rows(k), :], jnp.zeros((pad, KV_RANK), F32)], axis=0).astype(BF16)
        kn = jnp.concatenate([kn_ref[seq_rows(k), :][:, :ROPE_DIM], jnp.zeros((pad, ROPE_DIM), F32)],
                             axis=0).astype(BF16)
        s_new = (_dot_nt(qls[k], cn) + _dot_nt(qrs[k], kn)) * scale
        t_q = lax.broadcasted_iota(jnp.int32, s_new.shape, 0) % ts
        t_k = lax.broadcasted_iota(jnp.int32, s_new.shape, 1)
        s_new = jnp.where(t_k <= t_q, s_new, -jnp.inf)
        m = jnp.maximum(jnp.max(s_scr[k], axis=-1, keepdims=True), jnp.max(s_new, axis=-1, keepdims=True))
        p_new = jnp.exp(s_new - m)
        ms.append(m)
        ls.append(jnp.sum(p_new, axis=-1, keepdims=True))
        accs.append(_dot(p_new.astype(BF16), cn))
    for c in range(n_chunks):
        for k in range(spp):
            p = jnp.exp(s_scr[k, :, c * tk:(c + 1) * tk] - ms[k])
            ls[k] = ls[k] + jnp.sum(p, axis=-1, keepdims=True)
            accs[k] = accs[k] + _dot(p.astype(BF16), lat_chunk(k, c).astype(BF16))
    for k in range(spp):
        o_ref[:, seq_rows(k), :] = (accs[k] * (1.0 / ls[k])).reshape(heads, ts, KV_RANK)

    @pl.when(b == nb - 1)
    def _():
        wait_fetch(1 - slot)


def _attn_sample(page_table, ql, qr, ckv, kr, cache_lat, cache_rope_t, *, layer, ts, scale):
    heads, t, _ = ql.shape
    n_seq, n_pages = page_table.shape
    page = cache_lat.shape[2]
    n_past = n_pages * page
    ppc = min(PAST_CHUNK_PAGES, n_pages)
    spp = min(PAST_SEQS_PER_STEP, n_seq)
    rows = spp * ts
    grid_spec = pltpu.PrefetchScalarGridSpec(
        num_scalar_prefetch=1,
        grid=(n_seq // spp,),
        in_specs=[pl.BlockSpec((heads, rows, KV_RANK), lambda b, pt: (0, b, 0)),
                  pl.BlockSpec((heads, rows, LANES), lambda b, pt: (0, b, 0)),
                  pl.BlockSpec((rows, KV_RANK), lambda b, pt: (b, 0)),
                  pl.BlockSpec((rows, LANES), lambda b, pt: (b, 0)),
                  pl.BlockSpec(memory_space=pl.ANY),
                  pl.BlockSpec(memory_space=pl.ANY)],
        out_specs=pl.BlockSpec((heads, rows, KV_RANK), lambda b, pt: (0, b, 0)),
        scratch_shapes=[pltpu.VMEM((2, spp, n_past, KV_RANK), F32),
                        pltpu.VMEM((2, spp, n_pages, ROPE_DIM, page), F32),
                        pltpu.VMEM((spp, heads * ts, n_past), F32),
                        pltpu.SemaphoreType.DMA((2, 2))],
    )
    return pl.pallas_call(
        functools.partial(_attn_sample_kernel, layer=layer, n_pages=n_pages, page=page, ppc=ppc, ts=ts,
                          scale=scale),
        grid_spec=grid_spec,
        out_shape=jax.ShapeDtypeStruct((heads, t, KV_RANK), F32),
        compiler_params=_params("arbitrary"),
        name="mla_attn_sample",
    )(page_table, ql, qr, ckv, kr, cache_lat, cache_rope_t)


def _mla_sample_out_kernel(ol_ref, wuv_ref, wo_ref, x_ref, o_ref, os_scr):
    heads = ol_ref.shape[0]
    vd = wuv_ref.shape[2]
    for h in range(heads):
        os_scr[:, h * vd:(h + 1) * vd] = _dot(ol_ref[h].astype(BF16), wuv_ref[h]).astype(BF16)
    o_ref[...] = x_ref[...] + _dot(os_scr[...], wo_ref[...])


def _mla_sample_out(ol, wuv_h, wo, x):
    t, d = x.shape
    heads, _, vd = wuv_h.shape
    return pl.pallas_call(
        _mla_sample_out_kernel,
        grid=(1,),
        in_specs=[_const_spec(ol.shape), _const_spec(wuv_h.shape), _const_spec(wo.shape), _const_spec(x.shape)],
        out_specs=pl.BlockSpec((t, d), lambda i: (0, 0)),
        out_shape=jax.ShapeDtypeStruct((t, d), F32),
        scratch_shapes=[pltpu.VMEM((t, heads * vd), BF16)],
        compiler_params=_params("arbitrary"),
        name="mla_out_sample",
    )(ol, wuv_h, wo, x)


def _matmul_residual_kernel(a_ref, w_ref, x_ref, o_ref):
    o_ref[...] = x_ref[...] + _dot(a_ref[...].astype(BF16), w_ref[...])


def _matmul_residual(a, w, x, *, tm, layer=None):
    t, d = x.shape
    k = a.shape[1]
    return pl.pallas_call(
        _matmul_residual_kernel,
        grid=(t // tm,),
        in_specs=[pl.BlockSpec((tm, k), lambda i: (i, 0)), _weight_spec(w, layer),
                  pl.BlockSpec((tm, d), lambda i: (i, 0))],
        out_specs=pl.BlockSpec((tm, d), lambda i: (i, 0)),
        out_shape=jax.ShapeDtypeStruct((t, d), F32),
        compiler_params=_params("arbitrary"),
        name="matmul_residual",
    )(a, w, x)


def _mem_kv_kernel(mem_ref, g_ref, wk_ref, wv_ref, k_ref, v_ref, mn_scr):
    @pl.when(pl.program_id(1) == 0)
    def _():
        mn_scr[...] = _rms(mem_ref[...], g_ref[...]).astype(BF16)
    mn = mn_scr[...]
    k_ref[...] = _dot(mn, wk_ref[...])
    v_ref[...] = _dot(mn, wv_ref[...])


def _mem_kv(mem, g, wkv, *, tn=512):
    m, d = mem.shape
    depth = wkv.shape[0]
    e = wkv.shape[2] // 2
    nk = e // tn
    return pl.pallas_call(
        _mem_kv_kernel,
        grid=(depth, nk),
        in_specs=[_const_spec((m, d)),
                  pl.BlockSpec((None, 1, d), lambda l, n: (l, 0, 0)),
                  pl.BlockSpec((None, d, tn), lambda l, n: (l, 0, n)),
                  pl.BlockSpec((None, d, tn), lambda l, n: (l, 0, nk + n))],
        out_specs=[pl.BlockSpec((None, m, tn), lambda l, n: (l, 0, n)),
                   pl.BlockSpec((None, m, tn), lambda l, n: (l, 0, n))],
        out_shape=[jax.ShapeDtypeStruct((depth, m, e), F32), jax.ShapeDtypeStruct((depth, m, e), F32)],
        scratch_shapes=[pltpu.VMEM((m, d), BF16)],
        compiler_params=_params("arbitrary", "arbitrary"),
        name="mem_kv",
    )(mem, g, wkv, wkv)


def _xattn_heads(q, k_of, v_of, heads, scale):
    e = q.shape[1] // heads
    outs = []
    for h in range(heads):
        s = _dot_nt(q[:, h * e:(h + 1) * e].astype(BF16), k_of(h)) * scale
        p = jnp.exp(s - jnp.max(s, axis=-1, keepdims=True))
        p = p * (1.0 / jnp.sum(p, axis=-1, keepdims=True))
        outs.append(_dot(p.astype(BF16), v_of(h)))
    return outs


def _xattn_prompt_kernel(x_ref, g_ref, wq_ref, mk_ref, mv_ref, wo_ref, o_ref, o_scr, *, heads, scale):
    x = x_ref[...]
    q = _dot(_rms(x, g_ref[...]).astype(BF16), wq_ref[...])
    e = q.shape[1] // heads
    outs = _xattn_heads(q, lambda h: mk_ref[:, h * e:(h + 1) * e].astype(BF16),
                        lambda h: mv_ref[:, h * e:(h + 1) * e].astype(BF16), heads, scale)
    for h in range(heads):
        o_scr[:, h * e:(h + 1) * e] = outs[h].astype(BF16)
    o_ref[...] = x + _dot(o_scr[...], wo_ref[...])


def _xattn_prompt(x, g, wq, mk_all, mv_all, wo, *, layer, heads, seq_len, n_mem, scale):
    t, d = x.shape
    tm = min(PROMPT_TILE, seq_len)
    tps = seq_len // tm
    return pl.pallas_call(
        functools.partial(_xattn_prompt_kernel, heads=heads, scale=scale),
        grid=(t // tm,),
        in_specs=[pl.BlockSpec((tm, d), lambda i: (i, 0)),
                  _weight_spec(g, layer), _weight_spec(wq, layer),
                  pl.BlockSpec((None, n_mem, d), lambda i: (layer, i // tps, 0)),
                  pl.BlockSpec((None, n_mem, d), lambda i: (layer, i // tps, 0)),
                  _weight_spec(wo, layer)],
        out_specs=pl.BlockSpec((tm, d), lambda i: (i, 0)),
        out_shape=jax.ShapeDtypeStruct((t, d), F32),
        scratch_shapes=[pltpu.VMEM((tm, d), BF16)],
        compiler_params=_params("arbitrary"),
        name="xattn_prompt",
    )(x, g, wq, mk_all, mv_all, wo)


def _norm_matmul_kernel(x_ref, g_ref, w_ref, o_ref):
    o_ref[...] = _dot(_rms(x_ref[...], g_ref[...]).astype(BF16), w_ref[...])


def _norm_matmul(x, g, w, *, layer):
    t, d = x.shape
    n = w.shape[-1]
    return pl.pallas_call(
        _norm_matmul_kernel,
        grid=(1,),
        in_specs=[_const_spec((t, d)), _weight_spec(g, layer), _weight_spec(w, layer)],
        out_specs=pl.BlockSpec((t, n), lambda i: (0, 0)),
        out_shape=jax.ShapeDtypeStruct((t, n), F32),
        compiler_params=_params("arbitrary"),
        name="norm_matmul",
    )(x, g, w)


def _xattn_sample_kernel(q_ref, mk_ref, mv_ref, o_ref, *, ts, scale):
    nseq, _, heads, e = mk_ref.shape
    for s_i in range(nseq):
        q = q_ref[s_i * ts:(s_i + 1) * ts, :]
        q_hm = jnp.concatenate([q[:, h * e:(h + 1) * e] for h in range(heads)], axis=0).astype(BF16)
        k_all = mk_ref[s_i].reshape(-1, e).astype(BF16)
        v_all = mv_ref[s_i].reshape(-1, e).astype(BF16)
        s = _dot_nt(q_hm, k_all) * scale
        key_head = lax.broadcasted_iota(jnp.int32, s.shape, 1) % heads
        row_head = lax.broadcasted_iota(jnp.int32, s.shape, 0) // ts
        s = jnp.where(key_head == row_head, s, -jnp.inf)
        p = jnp.exp(s - jnp.max(s, axis=-1, keepdims=True))
        p = p * (1.0 / jnp.sum(p, axis=-1, keepdims=True))
        o = _dot(p.astype(BF16), v_all)
        for h in range(heads):
            o_ref[s_i * ts:(s_i + 1) * ts, h * e:(h + 1) * e] = o[h * ts:(h + 1) * ts]


def _xattn_sample(q, mk_all, mv_all, *, layer, ts, scale, seqs_per_step=4):
    t, d = q.shape
    _, n_seq, n_mem, heads, e = mk_all.shape
    sp = min(seqs_per_step, n_seq)
    cache_spec = pl.BlockSpec((None, sp, n_mem, heads, e), lambda i: (layer, i, 0, 0, 0))
    return pl.pallas_call(
        functools.partial(_xattn_sample_kernel, ts=ts, scale=scale),
        grid=(n_seq // sp,),
        in_specs=[pl.BlockSpec((sp * ts, d), lambda i: (i, 0)), cache_spec, cache_spec],
        out_specs=pl.BlockSpec((sp * ts, d), lambda i: (i, 0)),
        out_shape=jax.ShapeDtypeStruct((t, d), F32),
        compiler_params=_params("arbitrary"),
        name="xattn_sample",
    )(q, mk_all, mv_all)


def _rope_tables(pos):
    half = ROPE_DIM // 2
    inv = 1.0 / (ROPE_THETA ** (jnp.arange(half, dtype=F32) * (2.0 / ROPE_DIM)))
    ang = pos.astype(F32)[:, None] * inv[None, :]
    reps = LANES // half
    return jnp.tile(jnp.cos(ang), (1, reps)), jnp.tile(jnp.sin(ang), (1, reps))


def _rot_cols(w):
    half = w.shape[-1] // 2
    return jnp.concatenate([-w[..., half:], w[..., :half]], axis=-1)


def _pad_lanes(w):
    return jnp.pad(w, [(0, 0)] * (w.ndim - 1) + [(0, LANES - w.shape[-1])])


def _state_to_prefix8(state):
    return jnp.pad(state, ((0, 0), (SUBLANES - state.shape[1], 0), (0, 0)))


def kernel(x_prompt, x_sample, state_conv, cache_kv_latent, cache_k_rope, state_ffn_conv, cache_mem_k, cache_mem_v, page_table, mem_prompt, conv_norm_g, conv_w_in, conv_w, conv_w_out, mla_norm_g, mla_w_down, mla_q_norm_g, mla_kv_norm_g, mla_w_uq, mla_w_uk, mla_w_uv, mla_w_o, xa_norm_g, xa_mem_norm_g, xa_w_q, xa_w_kv, xa_w_o, ffn_norm_g, ffn_w_up, ffn_conv_w, ffn_w_down, final_norm_g):
    bp, sp, d = x_prompt.shape
    bs, ts, _ = x_sample.shape
    depth = ffn_w_up.shape[0]
    d_ff = ffn_w_down.shape[1]
    heads = mla_w_uq.shape[2]
    xa_heads, xa_dim = xa_w_q.shape[2], xa_w_q.shape[3]
    n_mem = mem_prompt.shape[1]
    n_pages, page = page_table.shape[1], cache_kv_latent.shape[2]
    score_scale = (NOPE_DIM + ROPE_DIM) ** -0.5
    xa_scale = xa_dim ** -0.5
    assert ts == SUBLANES and KV_RANK + Q_RANK + ROPE_DIM == mla_w_down.shape[2]

    xp = x_prompt.reshape(bp * sp, d)
    xs = x_sample.reshape(bs * ts, d)

    cos_p, sin_p = _rope_tables(jnp.arange(sp, dtype=jnp.int32))
    cos_s, sin_s = _rope_tables(n_pages * page + jnp.arange(ts, dtype=jnp.int32))
    cos_s, sin_s = jnp.tile(cos_s, (bs, 1)), jnp.tile(sin_s, (bs, 1))

    xa_e = xa_heads * xa_dim
    gain = lambda g: g.reshape(g.shape[0], 1, g.shape[1])
    conv_g, xa_g, ffn_g = gain(conv_norm_g), gain(xa_norm_g), gain(ffn_norm_g)
    w_in_all, w_out_all = conv_w_in.astype(BF16), conv_w_out.astype(BF16)
    wq_all = xa_w_q.reshape(depth, d, xa_e).astype(BF16)
    wo_x_all = xa_w_o.reshape(depth, xa_e, d).astype(BF16)
    w_up_all, w_dn_all = ffn_w_up.astype(BF16), ffn_w_down.astype(BF16)

    wkv_all = xa_w_kv.reshape(depth, d, 2 * xa_e).astype(BF16)
    mk_all, mv_all = _mem_kv(mem_prompt.reshape(bp * n_mem, d), gain(xa_mem_norm_g), wkv_all)
    cache_rope_t = jnp.swapaxes(cache_k_rope, 2, 3)

    conv_p, conv_s, lat_p, rop_p, lat_s, rop_s, ffn_p, ffn_s = [], [], [], [], [], [], [], []

    for i in range(depth):
        j = i // N_MIXERS
        if i % N_MIXERS == 0:
            xp, st = _conv_mixer(xp, None, conv_g, w_in_all, conv_w, w_out_all, layer=j, seq_len=sp)
            conv_p.append(st[:, SUBLANES - 2:, :])
            xs, st = _conv_mixer(xs, _state_to_prefix8(state_conv[j]), conv_g, w_in_all, conv_w, w_out_all,
                                 layer=j, seq_len=ts)
            conv_s.append(st[:, SUBLANES - 2:, :])
        else:
            wd = mla_w_down[j]
            w_r = wd[:, Q_RANK + KV_RANK:]
            wd_ext = jnp.concatenate([wd[:, :Q_RANK + KV_RANK], _pad_lanes(w_r), _pad_lanes(_rot_cols(w_r))],
                                     axis=1).astype(BF16)
            wuq = mla_w_uq[j]
            wuq_r = wuq[:, :, NOPE_DIM:]
            wuq_ext = jnp.concatenate([wuq[:, :, :NOPE_DIM].reshape(Q_RANK, heads * NOPE_DIM),
                                       _pad_lanes(wuq_r).reshape(Q_RANK, heads * LANES),
                                       _pad_lanes(_rot_cols(wuq_r)).reshape(Q_RANK, heads * LANES)],
                                      axis=1).astype(BF16)
            wuk = mla_w_uk[j].astype(BF16)
            wuv = mla_w_uv[j].astype(BF16)
            wo = mla_w_o[j].reshape(heads * NOPE_DIM, d).astype(BF16)

            q, ckv, kr, kn, vt, krb = _mla_proj(
                xp, mla_norm_g[j], wd_ext, mla_q_norm_g[j], mla_kv_norm_g[j], wuq_ext, cos_p, sin_p,
                [wuk.reshape(KV_RANK, heads * NOPE_DIM), wuv.reshape(KV_RANK, heads * NOPE_DIM).T],
                heads=heads, seq_len=sp, prompt=True, q_scale=score_scale * LOG2_E)
            op = _attn_prompt(q, kn, krb, vt, heads=heads, seq_len=sp)
            xp = _matmul_residual(op, wo, xp, tm=min(PROMPT_TILE, sp))
            lat_p.append(ckv.reshape(bp, sp, KV_RANK))
            rop_p.append(kr[:, :ROPE_DIM].reshape(bp, sp, ROPE_DIM))

            ql, qr, ckv_s, kr_s = _mla_proj(
                xs, mla_norm_g[j], wd_ext, mla_q_norm_g[j], mla_kv_norm_g[j], wuq_ext, cos_s, sin_s,
                [wuk.transpose(1, 2, 0)], heads=heads, seq_len=ts, prompt=False)
            ol = _attn_sample(page_table, ql, qr, ckv_s, kr_s, cache_kv_latent, cache_rope_t,
                              layer=j, ts=ts, scale=score_scale)
            xs = _mla_sample_out(ol, wuv.transpose(1, 0, 2), wo, xs)
            lat_s.append(ckv_s.reshape(bs, ts, KV_RANK))
            rop_s.append(kr_s[:, :ROPE_DIM].reshape(bs, ts, ROPE_DIM))

        xp = _xattn_prompt(xp, xa_g, wq_all, mk_all, mv_all, wo_x_all, layer=i, heads=xa_heads, seq_len=sp,
                           n_mem=n_mem, scale=xa_scale)
        qs = _norm_matmul(xs, xa_g, wq_all, layer=i)
        os_ = _xattn_sample(qs, cache_mem_k, cache_mem_v, layer=i, ts=ts, scale=xa_scale)
        xs = _matmul_residual(os_, wo_x_all, xs, tm=bs * ts, layer=i)

        fg = final_norm_g if i == depth - 1 else None
        xp, st = _conv_ffn(xp, None, ffn_g, w_up_all, ffn_conv_w, w_dn_all, fg, layer=i, seq_len=sp)
        ffn_p.append(st[:, SUBLANES - 2:, :])
        xs, st = _conv_ffn(xs, _state_to_prefix8(state_ffn_conv[i]), ffn_g, w_up_all, ffn_conv_w, w_dn_all,
                           fg, layer=i, seq_len=ts)
        ffn_s.append(st[:, SUBLANES - 2:, :])

    mem_shape = (depth, bp, n_mem, xa_heads, xa_dim)
    return (xp.reshape(bp, sp, d), xs.reshape(bs, ts, d),
            jnp.stack(conv_p), jnp.stack(conv_s),
            jnp.stack(lat_p), jnp.stack(rop_p), jnp.stack(lat_s), jnp.stack(rop_s),
            jnp.stack(ffn_p), jnp.stack(ffn_s),
            mk_all.reshape(mem_shape), mv_all.reshape(mem_shape))
```

```python
import functools
import math

import jax
import jax.numpy as jnp
from jax import lax
from jax.experimental import pallas as pl
from jax.experimental.pallas import tpu as pltpu

F32 = jnp.float32
BF16 = jnp.bfloat16

RMS_EPS = 1e-6
ROPE_THETA = 10000.0
N_MIXERS = 2
NOPE_DIM = 128
ROPE_DIM = 64
Q_RANK = 384
KV_RANK = 256

SUBLANES = 8
BF16_SUBLANES = 16
LANES = 128
MXU_WIDTH = 256
VMEM_LIMIT_BYTES = 56 * 1024 * 1024

PROMPT_TILE = 512
XATTN_TILE = 1024
CHUNK = MXU_WIDTH
ATTN_TILE = 512
ATTN_HEADS_PER_STEP = 4
PAST_SEQS_PER_STEP = 2
PAST_CHUNK_PAGES = 8
LOG2_E = math.log2(math.e)


def _dot(a, b):
    return jnp.dot(a, b, preferred_element_type=F32)


def _dot_nt(a, b):
    return lax.dot_general(a, b, (((1,), (1,)), ((), ())), preferred_element_type=F32)


def _rms(xf, g):
    y = xf * lax.rsqrt(jnp.mean(xf * xf, axis=-1, keepdims=True) + RMS_EPS)
    return y * g


def _params(*sem):
    return pltpu.CompilerParams(dimension_semantics=tuple(sem), vmem_limit_bytes=VMEM_LIMIT_BYTES)


def _const_spec(shape):
    nd = len(shape)
    return pl.BlockSpec(shape, lambda *_: (0,) * nd, pipeline_mode=pl.Buffered(1))


def _weight_spec(w, layer=None):
    if layer is None:
        return _const_spec(w.shape)
    nd = w.ndim - 1
    return pl.BlockSpec((None,) + w.shape[1:], lambda *_: (layer,) + (0,) * nd, pipeline_mode=pl.Buffered(1))


def _conv3(u3, prefix8, w, ext_ref):
    r = u3.shape[1]
    ext_ref[:, 0:SUBLANES, :] = prefix8
    ext_ref[:, SUBLANES:SUBLANES + r, :] = u3
    y = (ext_ref[:, SUBLANES - 2:SUBLANES - 2 + r, :] * w[0:1, :][None]
         + ext_ref[:, SUBLANES - 1:SUBLANES - 1 + r, :] * w[1:2, :][None]
         + u3 * w[2:3, :][None])
    return y, ext_ref[:, r:r + SUBLANES, :]


def _zero_at_sequence_start(carry_ref, tiles_per_seq):
    @pl.when(pl.program_id(0) % tiles_per_seq == 0)
    def _():
        carry_ref[...] = jnp.zeros(carry_ref.shape, carry_ref.dtype)


def _conv_mixer_kernel(*refs, groups, rows, tiles_per_seq, has_prefix):
    if has_prefix:
        x_ref, pre_ref, g_ref, win_ref, cw_ref, wout_ref, o_ref, st_ref, xn_scr, ext_scr, p_scr = refs
    else:
        x_ref, g_ref, win_ref, cw_ref, wout_ref, o_ref, st_ref, xn_scr, ext_scr, p_scr, carry_scr = refs
        _zero_at_sequence_start(carry_scr, tiles_per_seq)
    tm = groups * rows
    d = x_ref.shape[1]
    x = x_ref[...]
    xn_scr[...] = _rms(x, g_ref[...]).astype(BF16)
    for n in range(d // CHUNK):
        lo, hi = n * CHUNK, (n + 1) * CHUNK
        xn = xn_scr[...]
        b = _dot(xn, win_ref[:, lo:hi])
        c = _dot(xn, win_ref[:, d + lo:d + hi])
        h = _dot(xn, win_ref[:, 2 * d + lo:2 * d + hi])
        u3 = (c * h).reshape(groups, rows, CHUNK)
        if has_prefix:
            pre = pre_ref[:, :, lo:hi]
        else:
            pre = carry_scr[:, :, lo:hi]
        y3, new8 = _conv3(u3, pre, cw_ref[:, lo:hi], ext_scr)
        st_ref[:, :, lo:hi] = new8
        if not has_prefix:
            carry_scr[:, :, lo:hi] = new8
        p_scr[:, lo:hi] = (b * y3.reshape(tm, CHUNK)).astype(BF16)
    o_ref[...] = x + _dot(p_scr[...], wout_ref[...])


def _conv_mixer(x, prefix8, g, w_in, cw, w_out, *, layer, seq_len):
    t, d = x.shape
    has_prefix = prefix8 is not None
    if has_prefix:
        groups, rows, tm, tps, n_state = t // SUBLANES, SUBLANES, t, 1, t // SUBLANES
    else:
        tm = min(PROMPT_TILE, seq_len)
        groups, rows, tps, n_state = 1, tm, seq_len // tm, t // seq_len
    grid = (t // tm,)
    in_specs = [pl.BlockSpec((tm, d), lambda i: (i, 0))]
    args = [x]
    if has_prefix:
        in_specs.append(pl.BlockSpec((groups, SUBLANES, d), lambda i: (0, 0, 0)))
        args.append(prefix8)
    in_specs += [_weight_spec(w, layer) for w in (g, w_in, cw, w_out)]
    args += [g, w_in, cw, w_out]
    scratch = [pltpu.VMEM((tm, d), BF16),
               pltpu.VMEM((groups, SUBLANES + rows, CHUNK), F32),
               pltpu.VMEM((tm, d), BF16)]
    if not has_prefix:
        scratch.append(pltpu.VMEM((1, SUBLANES, d), F32))
    st_block = (groups, SUBLANES, d)
    return pl.pallas_call(
        functools.partial(_conv_mixer_kernel, groups=groups, rows=rows, tiles_per_seq=tps,
                          has_prefix=has_prefix),
        grid=grid,
        in_specs=in_specs,
        out_specs=[pl.BlockSpec((tm, d), lambda i: (i, 0)),
                   pl.BlockSpec(st_block, lambda i: (i // tps, 0, 0))],
        out_shape=[jax.ShapeDtypeStruct((t, d), F32),
                   jax.ShapeDtypeStruct((n_state, SUBLANES, d), F32)],
        scratch_shapes=scratch,
        compiler_params=_params("arbitrary"),
        name="conv_mixer_sample" if has_prefix else "conv_mixer_prompt",
    )(*args)


def _ffn_kernel(*refs, groups, rows, tiles_per_seq, has_prefix, final):
    refs = list(refs)
    x_ref = refs.pop(0)
    pre_ref = refs.pop(0) if has_prefix else None
    g_ref, wup_ref, cw_ref, wd_ref = refs[:4]
    refs = refs[4:]
    gf_ref = refs.pop(0) if final else None
    o_ref, st_ref, xn_scr, ext_scr, h_scr = refs[:5]
    carry_scr = None if has_prefix else refs[5]
    if not has_prefix:
        _zero_at_sequence_start(carry_scr, tiles_per_seq)
    tm = groups * rows
    f = wd_ref.shape[0]
    x = x_ref[...]
    xn_scr[...] = _rms(x, g_ref[...]).astype(BF16)
    for n in range(f // CHUNK):
        lo, hi = n * CHUNK, (n + 1) * CHUNK
        xn = xn_scr[...]
        ug = _dot(xn, wup_ref[:, lo:hi])
        uv = _dot(xn, wup_ref[:, f + lo:f + hi])
        if has_prefix:
            pre = pre_ref[:, :, lo:hi]
        else:
            pre = carry_scr[:, :, lo:hi]
        gc3, new8 = _conv3(ug.reshape(groups, rows, CHUNK), pre, cw_ref[:, lo:hi], ext_scr)
        st_ref[:, :, lo:hi] = new8
        if not has_prefix:
            carry_scr[:, :, lo:hi] = new8
        gc = gc3.reshape(tm, CHUNK)
        h_scr[:, lo:hi] = ((gc * jax.nn.sigmoid(gc)) * uv).astype(BF16)
    res = x + _dot(h_scr[...], wd_ref[...])
    o_ref[...] = _rms(res, gf_ref[...]) if final else res


def _conv_ffn(x, prefix8, g, w_up, cw, wd, final_g, *, layer, seq_len):
    t, d = x.shape
    f = wd.shape[1]
    has_prefix = prefix8 is not None
    final = final_g is not None
    if has_prefix:
        groups, rows, tm, tps, n_state = t // SUBLANES, SUBLANES, t, 1, t // SUBLANES
    else:
        tm = min(PROMPT_TILE, seq_len)
        groups, rows, tps, n_state = 1, tm, seq_len // tm, t // seq_len
    in_specs = [pl.BlockSpec((tm, d), lambda i: (i, 0))]
    args = [x]
    if has_prefix:
        in_specs.append(pl.BlockSpec((groups, SUBLANES, f), lambda i: (0, 0, 0)))
        args.append(prefix8)
    in_specs += [_weight_spec(w, layer) for w in (g, w_up, cw, wd)]
    args += [g, w_up, cw, wd]
    if final:
        in_specs.append(_const_spec((1, d)))
        args.append(final_g.reshape(1, d))
    scratch = [pltpu.VMEM((tm, d), BF16),
               pltpu.VMEM((groups, SUBLANES + rows, CHUNK), F32),
               pltpu.VMEM((tm, f), BF16)]
    if not has_prefix:
        scratch.append(pltpu.VMEM((1, SUBLANES, f), F32))
    return pl.pallas_call(
        functools.partial(_ffn_kernel, groups=groups, rows=rows, tiles_per_seq=tps,
                          has_prefix=has_prefix, final=final),
        grid=(t // tm,),
        in_specs=in_specs,
        out_specs=[pl.BlockSpec((tm, d), lambda i: (i, 0)),
                   pl.BlockSpec((groups, SUBLANES, f), lambda i: (i // tps, 0, 0))],
        out_shape=[jax.ShapeDtypeStruct((t, d), F32),
                   jax.ShapeDtypeStruct((n_state, SUBLANES, f), F32)],
        scratch_shapes=scratch,
        compiler_params=_params("arbitrary"),
        name="conv_ffn_sample" if has_prefix else "conv_ffn_prompt",
    )(*args)


def _mla_proj_kernel(*refs, heads, prompt, q_scale):
    if prompt:
        (x_ref, g_ref, wd_ref, qg_ref, kvg_ref, wuq_ref, cos_ref, sin_ref, wuk_ref, wuvt_ref,
         qn_ref, qrope_ref, ckv_ref, kr_ref, kn_ref, vt_ref, krb_ref) = refs
    else:
        (x_ref, g_ref, wd_ref, qg_ref, kvg_ref, wuq_ref, cos_ref, sin_ref, wukt_ref,
         ql_ref, qr_ref, ckv_ref, kr_ref) = refs
    hn = heads * NOPE_DIM
    cos = cos_ref[...]
    sin = sin_ref[...]
    xn = _rms(x_ref[...], g_ref[...]).astype(BF16)
    dn = _dot(xn, wd_ref[...])
    c_q = _rms(dn[:, :Q_RANK], qg_ref[...]).astype(BF16)
    c_kv = _rms(dn[:, Q_RANK:Q_RANK + KV_RANK], kvg_ref[...])
    ckv_ref[...] = c_kv
    r0 = Q_RANK + KV_RANK
    k_r = dn[:, r0:r0 + LANES] * cos + dn[:, r0 + LANES:r0 + 2 * LANES] * sin
    kr_ref[...] = k_r
    q = _dot(c_q, wuq_ref[...])
    if prompt:
        hr = heads * ROPE_DIM
        qn_ref[...] = (q[:, :hn] * q_scale).astype(BF16)
        for c in range(hr // LANES):
            lo = hn + c * LANES
            qr_c = q[:, lo:lo + LANES] * cos + q[:, lo + hr:lo + hr + LANES] * sin
            qrope_ref[:, c * LANES:(c + 1) * LANES] = (qr_c * q_scale).astype(BF16)
    else:
        for h in range(heads):
            lo = hn + h * LANES
            qr_ref[h] = q[:, lo:lo + LANES] * cos + q[:, lo + heads * LANES:lo + (heads + 1) * LANES] * sin
            qn_h = q[:, h * NOPE_DIM:(h + 1) * NOPE_DIM].astype(BF16)
            ql_ref[h] = _dot(qn_h, wukt_ref[h])
    if prompt:
        ckv_b = c_kv.astype(BF16)
        kn_ref[...] = _dot(ckv_b, wuk_ref[...]).astype(BF16)
        vt_ref[...] = _dot_nt(wuvt_ref[...], ckv_b).astype(BF16)
        krb_ref[...] = k_r.astype(BF16)


def _mla_proj(x, g, wd_ext, qg, kvg, wuq_ext, cos, sin, extra, *, heads, seq_len, prompt, q_scale=None):
    t, d = x.shape
    tm = min(PROMPT_TILE, seq_len) if prompt else t
    tps = seq_len // tm if prompt else 1
    hn = heads * NOPE_DIM
    in_specs = [pl.BlockSpec((tm, d), lambda i: (i, 0)),
                _const_spec((1, d)), _const_spec(wd_ext.shape), _const_spec((1, Q_RANK)),
                _const_spec((1, KV_RANK)), _const_spec(wuq_ext.shape),
                pl.BlockSpec((tm, LANES), lambda i: (i % tps, 0)),
                pl.BlockSpec((tm, LANES), lambda i: (i % tps, 0))]
    in_specs += [_const_spec(w.shape) for w in extra]
    row = lambda w: pl.BlockSpec((tm, w), lambda i: (i, 0))
    if prompt:
        hr = heads * ROPE_DIM
        out_specs = [row(hn), row(hr), row(KV_RANK), row(LANES), row(hn),
                     pl.BlockSpec((hn, tm), lambda i: (0, i)), row(LANES)]
        out_shape = [jax.ShapeDtypeStruct((t, hn), BF16), jax.ShapeDtypeStruct((t, hr), BF16),
                     jax.ShapeDtypeStruct((t, KV_RANK), F32),
                     jax.ShapeDtypeStruct((t, LANES), F32), jax.ShapeDtypeStruct((t, hn), BF16),
                     jax.ShapeDtypeStruct((hn, t), BF16), jax.ShapeDtypeStruct((t, LANES), BF16)]
    else:
        head_major = lambda w: pl.BlockSpec((heads, tm, w), lambda i: (0, i, 0))
        out_specs = [head_major(KV_RANK), head_major(LANES), row(KV_RANK), row(LANES)]
        out_shape = [jax.ShapeDtypeStruct((heads, t, KV_RANK), F32), jax.ShapeDtypeStruct((heads, t, LANES), F32),
                     jax.ShapeDtypeStruct((t, KV_RANK), F32), jax.ShapeDtypeStruct((t, LANES), F32)]
    return pl.pallas_call(
        functools.partial(_mla_proj_kernel, heads=heads, prompt=prompt, q_scale=q_scale),
        grid=(t // tm,),
        in_specs=in_specs,
        out_specs=out_specs,
        out_shape=out_shape,
        compiler_params=_params("arbitrary"),
        name="mla_proj_prompt" if prompt else "mla_proj_sample",
    )(x, g.reshape(1, d), wd_ext, qg.reshape(1, Q_RANK), kvg.reshape(1, KV_RANK), wuq_ext, cos, sin, *extra)


def _attn_prompt_kernel(qn_ref, qr_ref, kn_ref, kr_ref, vt_ref, o_ref,
                        sa_scr, sb_scr, m_scr, acc_scr, *, t):
    i = pl.program_id(2)
    hps = sa_scr.shape[0]
    head = lambda g: slice(g * NOPE_DIM, (g + 1) * NOPE_DIM)
    qs = [jnp.concatenate([qn_ref[:, head(g)], qr_ref[:, g * ROPE_DIM:(g + 1) * ROPE_DIM]], axis=1)
          for g in range(hps)]
    m_scr[...] = jnp.full(m_scr.shape, -jnp.inf, F32)
    acc_scr[...] = jnp.zeros(acc_scr.shape, F32)
    ones = jnp.ones((BF16_SUBLANES, t), BF16)

    def scores(j, dst):
        ks = pl.multiple_of(j * t, t)
        kr = kr_ref[pl.ds(ks, t), :ROPE_DIM]
        for g in range(hps):
            k = jnp.concatenate([kn_ref[pl.ds(ks, t), head(g)], kr], axis=1)
            dst[g] = _dot_nt(k, qs[g])

    def update(src, j, diagonal):
        ks = pl.multiple_of(j * t, t)
        for g in range(hps):
            s = src[g]
            if diagonal:
                kpos = lax.broadcasted_iota(jnp.int32, (t, t), 0)
                qpos = lax.broadcasted_iota(jnp.int32, (t, t), 1)
                s = jnp.where(kpos <= qpos, s, -jnp.inf)
            m = m_scr[g]
            m_new = jnp.maximum(m, jnp.max(s, axis=0, keepdims=True))
            alpha = jnp.exp2(m - m_new)
            p = jnp.exp2(s - m_new).astype(BF16)
            v1 = jnp.concatenate([vt_ref[head(g), pl.ds(ks, t)], ones], axis=0)
            acc_scr[g] = alpha * acc_scr[g] + _dot(v1, p)
            m_scr[g] = m_new

    scores(0, sa_scr)

    def pair(jj, _):
        scores(2 * jj + 1, sb_scr)
        update(sa_scr, 2 * jj, False)
        scores(2 * jj + 2, sa_scr)
        update(sb_scr, 2 * jj + 1, False)
        return 0
    lax.fori_loop(0, i // 2, pair, 0)

    @pl.when(i % 2 == 0)
    def _():
        update(sa_scr, i, True)

    @pl.when(i % 2 == 1)
    def _():
        scores(i, sb_scr)
        update(sa_scr, i - 1, False)
        update(sb_scr, i, True)

    for g in range(hps):
        l = acc_scr[g, NOPE_DIM:NOPE_DIM + 1, :]
        o_ref[:, head(g)] = (acc_scr[g, :NOPE_DIM, :] * (1.0 / l)).T.astype(o_ref.dtype)


def _attn_prompt(qn, qrope, kn, krb, vt, *, heads, seq_len):
    tokens = qn.shape[0]
    bsz = tokens // seq_len
    t = min(ATTN_TILE, seq_len)
    nq = seq_len // t
    hps = ATTN_HEADS_PER_STEP
    hg = heads // hps
    w = hps * NOPE_DIM
    return pl.pallas_call(
        functools.partial(_attn_prompt_kernel, t=t),
        grid=(bsz, hg, nq),
        in_specs=[pl.BlockSpec((t, w), lambda b, h, i: (b * nq + i, h)),
                  pl.BlockSpec((t, hps * ROPE_DIM), lambda b, h, i: (b * nq + i, h)),
                  pl.BlockSpec((seq_len, w), lambda b, h, i: (b, h)),
                  pl.BlockSpec((seq_len, LANES), lambda b, h, i: (b, 0)),
                  pl.BlockSpec((w, seq_len), lambda b, h, i: (h, b))],
        out_specs=pl.BlockSpec((t, w), lambda b, h, i: (b * nq + i, h)),
        out_shape=jax.ShapeDtypeStruct((tokens, heads * NOPE_DIM), BF16),
        scratch_shapes=[pltpu.VMEM((hps, t, t), F32), pltpu.VMEM((hps, t, t), F32), pltpu.VMEM((hps, 1, t), F32),
                        pltpu.VMEM((hps, NOPE_DIM + BF16_SUBLANES, t), F32)],
        compiler_params=_params("arbitrary", "arbitrary", "arbitrary"),
        name="mla_attn_prompt",
    )(qn, qrope, kn, krb, vt)


def _attn_sample_kernel(pt_ref, ql_ref, qr_ref, cn_ref, kn_ref, lat_hbm, ropet_hbm, o_ref,
                        lat_buf, rope_buf, s_scr, sem, *, layer, n_pages, page, ppc, ts, scale):
    b = pl.program_id(0)
    nb = pl.num_programs(0)
    slot = b % 2
    heads = ql_ref.shape[0]
    spp = s_scr.shape[0]
    tk = ppc * page
    n_chunks = n_pages // ppc
    seq_rows = lambda k: slice(k * ts, (k + 1) * ts)

    def start_fetch(group, dst_slot):
        for k in range(spp):
            for p in range(n_pages):
                pg = pt_ref[group * spp + k, p]
                pltpu.make_async_copy(lat_hbm.at[layer, pg], lat_buf.at[dst_slot, k, p * page:(p + 1) * page, :],
                                      sem.at[0, dst_slot]).start()
                pltpu.make_async_copy(ropet_hbm.at[layer, pg], rope_buf.at[dst_slot, k, p],
                                      sem.at[1, dst_slot]).start()

    def wait_fetch(dst_slot):
        pltpu.make_async_copy(lat_buf.at[dst_slot], lat_buf.at[dst_slot], sem.at[0, dst_slot]).wait()
        pltpu.make_async_copy(rope_buf.at[dst_slot], rope_buf.at[dst_slot], sem.at[1, dst_slot]).wait()

    @pl.when(b == 0)
    def _():
        start_fetch(0, 0)

    wait_fetch(slot)
    start_fetch((b + 1) % nb, 1 - slot)

    qls = [ql_ref[:, seq_rows(k), :].reshape(heads * ts, KV_RANK).astype(BF16) for k in range(spp)]
    qrs = [qr_ref[:, seq_rows(k), :].reshape(heads * ts, LANES)[:, :ROPE_DIM].astype(BF16) for k in range(spp)]

    def lat_chunk(k, c):
        return lat_buf[slot, k, c * tk:(c + 1) * tk, :]

    for c in range(n_chunks):
        for k in range(spp):
            ropt = jnp.concatenate([rope_buf[slot, k, c * ppc + j] for j in range(ppc)], axis=1).astype(BF16)
            lat_t = lat_chunk(k, c).T.astype(BF16)
            s_scr[k, :, c * tk:(c + 1) * tk] = (_dot(qls[k], lat_t) + _dot(qrs[k], ropt)) * scale

    pad = BF16_SUBLANES - ts
    ms, ls, accs = [], [], []
    for k in range(spp):
        cn = jnp.concatenate([cn_ref[seq_rows(k), :], jnp.zeros((pad, KV_RANK), F32)], axis=0).astype(BF16)
        kn = jnp.concatenate([kn_ref[seq_rows(k), :][:, :ROPE_DIM], jnp.zeros((pad, ROPE_DIM), F32)],
                             axis=0).astype(BF16)
        s_new = (_dot_nt(qls[k], cn) + _dot_nt(qrs[k], kn)) * scale
        t_q = lax.broadcasted_iota(jnp.int32, s_new.shape, 0) % ts
        t_k = lax.broadcasted_iota(jnp.int32, s_new.shape, 1)
        s_new = jnp.where(t_k <= t_q, s_new, -jnp.inf)
        m = jnp.maximum(jnp.max(s_scr[k], axis=-1, keepdims=True), jnp.max(s_new, axis=-1, keepdims=True))
        p_new = jnp.exp(s_new - m)
        ms.append(m)
        ls.append(jnp.sum(p_new, axis=-1, keepdims=True))
        accs.append(_dot(p_new.astype(BF16), cn))
    for c in range(n_chunks):
        for k in range(spp):
            p = jnp.exp(s_scr[k, :, c * tk:(c + 1) * tk] - ms[k])
            ls[k] = ls[k] + jnp.sum(p, axis=-1, keepdims=True)
            accs[k] = accs[k] + _dot(p.astype(BF16), lat_chunk(k, c).astype(BF16))
    for k in range(spp):
        o_ref[:, seq_rows(k), :] = (accs[k] * (1.0 / ls[k])).reshape(heads, ts, KV_RANK)

    @pl.when(b == nb - 1)
    def _():
        wait_fetch(1 - slot)


def _attn_sample(page_table, ql, qr, ckv, kr, cache_lat, cache_rope_t, *, layer, ts, scale):
    heads, t, _ = ql.shape
    n_seq, n_pages = page_table.shape
    page = cache_lat.shape[2]
    n_past = n_pages * page
    ppc = min(PAST_CHUNK_PAGES, n_pages)
    spp = min(PAST_SEQS_PER_STEP, n_seq)
    rows = spp * ts
    grid_spec = pltpu.PrefetchScalarGridSpec(
        num_scalar_prefetch=1,
        grid=(n_seq // spp,),
        in_specs=[pl.BlockSpec((heads, rows, KV_RANK), lambda b, pt: (0, b, 0)),
                  pl.BlockSpec((heads, rows, LANES), lambda b, pt: (0, b, 0)),
                  pl.BlockSpec((rows, KV_RANK), lambda b, pt: (b, 0)),
                  pl.BlockSpec((rows, LANES), lambda b, pt: (b, 0)),
                  pl.BlockSpec(memory_space=pl.ANY),
                  pl.BlockSpec(memory_space=pl.ANY)],
        out_specs=pl.BlockSpec((heads, rows, KV_RANK), lambda b, pt: (0, b, 0)),
        scratch_shapes=[pltpu.VMEM((2, spp, n_past, KV_RANK), F32),
                        pltpu.VMEM((2, spp, n_pages, ROPE_DIM, page), F32),
                        pltpu.VMEM((spp, heads * ts, n_past), F32),
                        pltpu.SemaphoreType.DMA((2, 2))],
    )
    return pl.pallas_call(
        functools.partial(_attn_sample_kernel, layer=layer, n_pages=n_pages, page=page, ppc=ppc, ts=ts,
                          scale=scale),
        grid_spec=grid_spec,
        out_shape=jax.ShapeDtypeStruct((heads, t, KV_RANK), F32),
        compiler_params=_params("arbitrary"),
        name="mla_attn_sample",
    )(page_table, ql, qr, ckv, kr, cache_lat, cache_rope_t)


def _mla_sample_out_kernel(ol_ref, wuv_ref, wo_ref, x_ref, o_ref, os_scr):
    heads = ol_ref.shape[0]
    vd = wuv_ref.shape[2]
    for h in range(heads):
        os_scr[:, h * vd:(h + 1) * vd] = _dot(ol_ref[h].astype(BF16), wuv_ref[h]).astype(BF16)
    o_ref[...] = x_ref[...] + _dot(os_scr[...], wo_ref[...])


def _mla_sample_out(ol, wuv_h, wo, x):
    t, d = x.shape
    heads, _, vd = wuv_h.shape
    return pl.pallas_call(
        _mla_sample_out_kernel,
        grid=(1,),
        in_specs=[_const_spec(ol.shape), _const_spec(wuv_h.shape), _const_spec(wo.shape), _const_spec(x.shape)],
        out_specs=pl.BlockSpec((t, d), lambda i: (0, 0)),
        out_shape=jax.ShapeDtypeStruct((t, d), F32),
        scratch_shapes=[pltpu.VMEM((t, heads * vd), BF16)],
        compiler_params=_params("arbitrary"),
        name="mla_out_sample",
    )(ol, wuv_h, wo, x)


def _matmul_residual_kernel(a_ref, w_ref, x_ref, o_ref):
    o_ref[...] = x_ref[...] + _dot(a_ref[...].astype(BF16), w_ref[...])


def _matmul_residual(a, w, x, *, tm, layer=None):
    t, d = x.shape
    k = a.shape[1]
    return pl.pallas_call(
        _matmul_residual_kernel,
        grid=(t // tm,),
        in_specs=[pl.BlockSpec((tm, k), lambda i: (i, 0)), _weight_spec(w, layer),
                  pl.BlockSpec((tm, d), lambda i: (i, 0))],
        out_specs=pl.BlockSpec((tm, d), lambda i: (i, 0)),
        out_shape=jax.ShapeDtypeStruct((t, d), F32),
        compiler_params=_params("arbitrary"),
        name="matmul_residual",
    )(a, w, x)


def _mem_kv_kernel(mem_ref, g_ref, wk_ref, wv_ref, k_ref, v_ref, mn_scr):
    @pl.when(pl.program_id(1) == 0)
    def _():
        mn_scr[...] = _rms(mem_ref[...], g_ref[...]).astype(BF16)
    mn = mn_scr[...]
    k_ref[...] = _dot(mn, wk_ref[...])
    v_ref[...] = _dot(mn, wv_ref[...])


def _mem_kv(mem, g, wkv, *, tn=512):
    m, d = mem.shape
    depth = wkv.shape[0]
    e = wkv.shape[2] // 2
    nk = e // tn
    return pl.pallas_call(
        _mem_kv_kernel,
        grid=(depth, nk),
        in_specs=[_const_spec((m, d)),
                  pl.BlockSpec((None, 1, d), lambda l, n: (l, 0, 0)),
                  pl.BlockSpec((None, d, tn), lambda l, n: (l, 0, n)),
                  pl.BlockSpec((None, d, tn), lambda l, n: (l, 0, nk + n))],
        out_specs=[pl.BlockSpec((None, m, tn), lambda l, n: (l, 0, n)),
                   pl.BlockSpec((None, m, tn), lambda l, n: (l, 0, n))],
        out_shape=[jax.ShapeDtypeStruct((depth, m, e), F32), jax.ShapeDtypeStruct((depth, m, e), F32)],
        scratch_shapes=[pltpu.VMEM((m, d), BF16)],
        compiler_params=_params("arbitrary", "arbitrary"),
        name="mem_kv",
    )(mem, g, wkv, wkv)


def _xattn_heads(q, k_of, v_of, heads, scale):
    e = q.shape[1] // heads
    outs = []
    for h in range(heads):
        s = _dot_nt(q[:, h * e:(h + 1) * e].astype(BF16), k_of(h)) * scale
        p = jnp.exp(s - jnp.max(s, axis=-1, keepdims=True))
        p = p * (1.0 / jnp.sum(p, axis=-1, keepdims=True))
        outs.append(_dot(p.astype(BF16), v_of(h)))
    return outs


def _xattn_prompt_kernel(*refs, heads, scale, has_mixer_out):
    if has_mixer_out:
        x_ref, a_ref, wa_ref, g_ref, wq_ref, mk_ref, mv_ref, wo_ref, o_ref, o_scr = refs
        x = x_ref[...] + _dot(a_ref[...], wa_ref[...])
    else:
        x_ref, g_ref, wq_ref, mk_ref, mv_ref, wo_ref, o_ref, o_scr = refs
        x = x_ref[...]
    q = _dot(_rms(x, g_ref[...]).astype(BF16), wq_ref[...])
    e = q.shape[1] // heads
    outs = _xattn_heads(q, lambda h: mk_ref[:, h * e:(h + 1) * e].astype(BF16),
                        lambda h: mv_ref[:, h * e:(h + 1) * e].astype(BF16), heads, scale)
    for h in range(heads):
        o_scr[:, h * e:(h + 1) * e] = outs[h].astype(BF16)
    o_ref[...] = x + _dot(o_scr[...], wo_ref[...])


def _xattn_prompt(x, g, wq, mk_all, mv_all, wo, *, layer, heads, seq_len, n_mem, scale, mixer_out=None):
    t, d = x.shape
    tm = min(XATTN_TILE, seq_len)
    tps = seq_len // tm
    row = lambda w: pl.BlockSpec((tm, w), lambda i: (i, 0))
    in_specs, args = [row(d)], [x]
    if mixer_out is not None:
        a, w_a = mixer_out
        in_specs += [row(a.shape[1]), _weight_spec(w_a)]
        args += [a, w_a]
    in_specs += [_weight_spec(g, layer), _weight_spec(wq, layer),
                 pl.BlockSpec((None, n_mem, d), lambda i: (layer, i // tps, 0)),
                 pl.BlockSpec((None, n_mem, d), lambda i: (layer, i // tps, 0)),
                 _weight_spec(wo, layer)]
    args += [g, wq, mk_all, mv_all, wo]
    return pl.pallas_call(
        functools.partial(_xattn_prompt_kernel, heads=heads, scale=scale, has_mixer_out=mixer_out is not None),
        grid=(t // tm,),
        in_specs=in_specs,
        out_specs=row(d),
        out_shape=jax.ShapeDtypeStruct((t, d), F32),
        scratch_shapes=[pltpu.VMEM((tm, d), BF16)],
        compiler_params=_params("arbitrary"),
        name="xattn_prompt",
    )(*args)


def _norm_matmul_kernel(x_ref, g_ref, w_ref, o_ref):
    o_ref[...] = _dot(_rms(x_ref[...], g_ref[...]).astype(BF16), w_ref[...])


def _norm_matmul(x, g, w, *, layer):
    t, d = x.shape
    n = w.shape[-1]
    return pl.pallas_call(
        _norm_matmul_kernel,
        grid=(1,),
        in_specs=[_const_spec((t, d)), _weight_spec(g, layer), _weight_spec(w, layer)],
        out_specs=pl.BlockSpec((t, n), lambda i: (0, 0)),
        out_shape=jax.ShapeDtypeStruct((t, n), F32),
        compiler_params=_params("arbitrary"),
        name="norm_matmul",
    )(x, g, w)


def _xattn_sample_kernel(q_ref, mk_ref, mv_ref, o_ref, *, ts, scale):
    nseq, _, heads, e = mk_ref.shape
    for s_i in range(nseq):
        q = q_ref[s_i * ts:(s_i + 1) * ts, :]
        q_hm = jnp.concatenate([q[:, h * e:(h + 1) * e] for h in range(heads)], axis=0).astype(BF16)
        k_all = mk_ref[s_i].reshape(-1, e).astype(BF16)
        v_all = mv_ref[s_i].reshape(-1, e).astype(BF16)
        s = _dot_nt(q_hm, k_all) * scale
        key_head = lax.broadcasted_iota(jnp.int32, s.shape, 1) % heads
        row_head = lax.broadcasted_iota(jnp.int32, s.shape, 0) // ts
        s = jnp.where(key_head == row_head, s, -jnp.inf)
        p = jnp.exp(s - jnp.max(s, axis=-1, keepdims=True))
        p = p * (1.0 / jnp.sum(p, axis=-1, keepdims=True))
        o = _dot(p.astype(BF16), v_all)
        for h in range(heads):
            o_ref[s_i * ts:(s_i + 1) * ts, h * e:(h + 1) * e] = o[h * ts:(h + 1) * ts]


def _xattn_sample(q, mk_all, mv_all, *, layer, ts, scale, seqs_per_step=8):
    t, d = q.shape
    _, n_seq, n_mem, heads, e = mk_all.shape
    sp = min(seqs_per_step, n_seq)
    cache_spec = pl.BlockSpec((None, sp, n_mem, heads, e), lambda i: (layer, i, 0, 0, 0))
    return pl.pallas_call(
        functools.partial(_xattn_sample_kernel, ts=ts, scale=scale),
        grid=(n_seq // sp,),
        in_specs=[pl.BlockSpec((sp * ts, d), lambda i: (i, 0)), cache_spec, cache_spec],
        out_specs=pl.BlockSpec((sp * ts, d), lambda i: (i, 0)),
        out_shape=jax.ShapeDtypeStruct((t, d), F32),
        compiler_params=_params("arbitrary"),
        name="xattn_sample",
    )(q, mk_all, mv_all)


def _rope_tables(pos):
    half = ROPE_DIM // 2
    inv = 1.0 / (ROPE_THETA ** (jnp.arange(half, dtype=F32) * (2.0 / ROPE_DIM)))
    ang = pos.astype(F32)[:, None] * inv[None, :]
    reps = LANES // half
    return jnp.tile(jnp.cos(ang), (1, reps)), jnp.tile(jnp.sin(ang), (1, reps))


def _rot_cols(w):
    half = w.shape[-1] // 2
    return jnp.concatenate([-w[..., half:], w[..., :half]], axis=-1)


def _pad_lanes(w):
    return jnp.pad(w, [(0, 0)] * (w.ndim - 1) + [(0, LANES - w.shape[-1])])


def _state_to_prefix8(state):
    return jnp.pad(state, ((0, 0), (SUBLANES - state.shape[1], 0), (0, 0)))


def kernel(x_prompt, x_sample, state_conv, cache_kv_latent, cache_k_rope, state_ffn_conv, cache_mem_k, cache_mem_v, page_table, mem_prompt, conv_norm_g, conv_w_in, conv_w, conv_w_out, mla_norm_g, mla_w_down, mla_q_norm_g, mla_kv_norm_g, mla_w_uq, mla_w_uk, mla_w_uv, mla_w_o, xa_norm_g, xa_mem_norm_g, xa_w_q, xa_w_kv, xa_w_o, ffn_norm_g, ffn_w_up, ffn_conv_w, ffn_w_down, final_norm_g):
    bp, sp, d = x_prompt.shape
    bs, ts, _ = x_sample.shape
    depth = ffn_w_up.shape[0]
    d_ff = ffn_w_down.shape[1]
    heads = mla_w_uq.shape[2]
    xa_heads, xa_dim = xa_w_q.shape[2], xa_w_q.shape[3]
    n_mem = mem_prompt.shape[1]
    n_pages, page = page_table.shape[1], cache_kv_latent.shape[2]
    score_scale = (NOPE_DIM + ROPE_DIM) ** -0.5
    xa_scale = xa_dim ** -0.5
    assert ts == SUBLANES and KV_RANK + Q_RANK + ROPE_DIM == mla_w_down.shape[2]

    xp = x_prompt.reshape(bp * sp, d)
    xs = x_sample.reshape(bs * ts, d)

    cos_p, sin_p = _rope_tables(jnp.arange(sp, dtype=jnp.int32))
    cos_s, sin_s = _rope_tables(n_pages * page + jnp.arange(ts, dtype=jnp.int32))
    cos_s, sin_s = jnp.tile(cos_s, (bs, 1)), jnp.tile(sin_s, (bs, 1))

    xa_e = xa_heads * xa_dim
    gain = lambda g: g.reshape(g.shape[0], 1, g.shape[1])
    conv_g, xa_g, ffn_g = gain(conv_norm_g), gain(xa_norm_g), gain(ffn_norm_g)
    w_in_all, w_out_all = conv_w_in.astype(BF16), conv_w_out.astype(BF16)
    wq_all = xa_w_q.reshape(depth, d, xa_e).astype(BF16)
    wo_x_all = xa_w_o.reshape(depth, xa_e, d).astype(BF16)
    w_up_all, w_dn_all = ffn_w_up.astype(BF16), ffn_w_down.astype(BF16)

    wkv_all = xa_w_kv.reshape(depth, d, 2 * xa_e).astype(BF16)
    mk_all, mv_all = _mem_kv(mem_prompt.reshape(bp * n_mem, d), gain(xa_mem_norm_g), wkv_all)
    cache_rope_t = jnp.swapaxes(cache_k_rope, 2, 3)

    conv_p, conv_s, lat_p, rop_p, lat_s, rop_s, ffn_p, ffn_s = [], [], [], [], [], [], [], []

    for i in range(depth):
        j = i // N_MIXERS
        prompt_mixer_out = None
        if i % N_MIXERS == 0:
            xp, st = _conv_mixer(xp, None, conv_g, w_in_all, conv_w, w_out_all, layer=j, seq_len=sp)
            conv_p.append(st[:, SUBLANES - 2:, :])
            xs, st = _conv_mixer(xs, _state_to_prefix8(state_conv[j]), conv_g, w_in_all, conv_w, w_out_all,
                                 layer=j, seq_len=ts)
            conv_s.append(st[:, SUBLANES - 2:, :])
        else:
            wd = mla_w_down[j]
            w_r = wd[:, Q_RANK + KV_RANK:]
            wd_ext = jnp.concatenate([wd[:, :Q_RANK + KV_RANK], _pad_lanes(w_r), _pad_lanes(_rot_cols(w_r))],
                                     axis=1).astype(BF16)
            wuq = mla_w_uq[j]
            wuq_r = wuq[:, :, NOPE_DIM:]
            wuq_n = wuq[:, :, :NOPE_DIM].reshape(Q_RANK, heads * NOPE_DIM)
            wuq_ext = jnp.concatenate([wuq_n, _pad_lanes(wuq_r).reshape(Q_RANK, heads * LANES),
                                       _pad_lanes(_rot_cols(wuq_r)).reshape(Q_RANK, heads * LANES)],
                                      axis=1).astype(BF16)
            wuq_packed = jnp.concatenate([wuq_n, wuq_r.reshape(Q_RANK, heads * ROPE_DIM),
                                          _rot_cols(wuq_r).reshape(Q_RANK, heads * ROPE_DIM)], axis=1).astype(BF16)
            wuk = mla_w_uk[j].astype(BF16)
            wuv = mla_w_uv[j].astype(BF16)
            wo = mla_w_o[j].reshape(heads * NOPE_DIM, d).astype(BF16)

            qn, qrope, ckv, kr, kn, vt, krb = _mla_proj(
                xp, mla_norm_g[j], wd_ext, mla_q_norm_g[j], mla_kv_norm_g[j], wuq_packed, cos_p, sin_p,
                [wuk.reshape(KV_RANK, heads * NOPE_DIM), wuv.reshape(KV_RANK, heads * NOPE_DIM).T],
                heads=heads, seq_len=sp, prompt=True, q_scale=score_scale * LOG2_E)
            op = _attn_prompt(qn, qrope, kn, krb, vt, heads=heads, seq_len=sp)
            prompt_mixer_out = (op, wo)
            lat_p.append(ckv.reshape(bp, sp, KV_RANK))
            rop_p.append(kr[:, :ROPE_DIM].reshape(bp, sp, ROPE_DIM))

            ql, qr, ckv_s, kr_s = _mla_proj(
                xs, mla_norm_g[j], wd_ext, mla_q_norm_g[j], mla_kv_norm_g[j], wuq_ext, cos_s, sin_s,
                [wuk.transpose(1, 2, 0)], heads=heads, seq_len=ts, prompt=False)
            ol = _attn_sample(page_table, ql, qr, ckv_s, kr_s, cache_kv_latent, cache_rope_t,
                              layer=j, ts=ts, scale=score_scale)
            xs = _mla_sample_out(ol, wuv.transpose(1, 0, 2), wo, xs)
            lat_s.append(ckv_s.reshape(bs, ts, KV_RANK))
            rop_s.append(kr_s[:, :ROPE_DIM].reshape(bs, ts, ROPE_DIM))

        xp = _xattn_prompt(xp, xa_g, wq_all, mk_all, mv_all, wo_x_all, layer=i, heads=xa_heads, seq_len=sp,
                           n_mem=n_mem, scale=xa_scale, mixer_out=prompt_mixer_out)
        qs = _norm_matmul(xs, xa_g, wq_all, layer=i)
        os_ = _xattn_sample(qs, cache_mem_k, cache_mem_v, layer=i, ts=ts, scale=xa_scale)
        xs = _matmul_residual(os_, wo_x_all, xs, tm=bs * ts, layer=i)

        fg = final_norm_g if i == depth - 1 else None
        xp, st = _conv_ffn(xp, None, ffn_g, w_up_all, ffn_conv_w, w_dn_all, fg, layer=i, seq_len=sp)
        ffn_p.append(st[:, SUBLANES - 2:, :])
        xs, st = _conv_ffn(xs, _state_to_prefix8(state_ffn_conv[i]), ffn_g, w_up_all, ffn_conv_w, w_dn_all,
                           fg, layer=i, seq_len=ts)
        ffn_s.append(st[:, SUBLANES - 2:, :])

    mem_shape = (depth, bp, n_mem, xa_heads, xa_dim)
    return (xp.reshape(bp, sp, d), xs.reshape(bs, ts, d),
            jnp.stack(conv_p), jnp.stack(conv_s),
            jnp.stack(lat_p), jnp.stack(rop_p), jnp.stack(lat_s), jnp.stack(rop_s),
            jnp.stack(ffn_p), jnp.stack(ffn_s),
            mk_all.reshape(mem_shape), mv_all.reshape(mem_shape))
```

```python
import functools
import math

import jax
import jax.numpy as jnp
from jax import lax
from jax.experimental import pallas as pl
from jax.experimental.pallas import tpu as pltpu

F32 = jnp.float32
BF16 = jnp.bfloat16

RMS_EPS = 1e-6
ROPE_THETA = 10000.0
N_MIXERS = 2
NOPE_DIM = 128
ROPE_DIM = 64
Q_RANK = 384
KV_RANK = 256

SUBLANES = 8
BF16_SUBLANES = 16
LANES = 128
MXU_WIDTH = 256
VMEM_LIMIT_BYTES = 56 * 1024 * 1024

PROMPT_TILE = 512
XATTN_TILE = 1024
CHUNK = MXU_WIDTH
ATTN_TILE = 512
ATTN_HEADS_PER_STEP = 4
PAST_SEQS_PER_STEP = 2
PAST_CHUNK_PAGES = 8
LOG2_E = math.log2(math.e)


def _dot(a, b):
    return jnp.dot(a, b, preferred_element_type=F32)


def _dot_nt(a, b):
    return lax.dot_general(a, b, (((1,), (1,)), ((), ())), preferred_element_type=F32)


def _rms(xf, g):
    y = xf * lax.rsqrt(jnp.mean(xf * xf, axis=-1, keepdims=True) + RMS_EPS)
    return y * g


def _params(*sem):
    return pltpu.CompilerParams(dimension_semantics=tuple(sem), vmem_limit_bytes=VMEM_LIMIT_BYTES)


def _const_spec(shape):
    nd = len(shape)
    return pl.BlockSpec(shape, lambda *_: (0,) * nd, pipeline_mode=pl.Buffered(1))


def _weight_spec(w, layer=None):
    if layer is None:
        return _const_spec(w.shape)
    nd = w.ndim - 1
    return pl.BlockSpec((None,) + w.shape[1:], lambda *_: (layer,) + (0,) * nd, pipeline_mode=pl.Buffered(1))


def _conv3(u3, prefix, w, ext_ref):
    r, p = u3.shape[1], prefix.shape[1]
    ext_ref[:, SUBLANES - p:SUBLANES, :] = prefix
    ext_ref[:, SUBLANES:SUBLANES + r, :] = u3
    y = (ext_ref[:, SUBLANES - 2:SUBLANES - 2 + r, :] * w[0:1, :][None]
         + ext_ref[:, SUBLANES - 1:SUBLANES - 1 + r, :] * w[1:2, :][None]
         + u3 * w[2:3, :][None])
    return y, ext_ref[:, r + SUBLANES - p:r + SUBLANES, :]


def _zero_at_sequence_start(carry_ref, tiles_per_seq):
    @pl.when(pl.program_id(0) % tiles_per_seq == 0)
    def _():
        carry_ref[...] = jnp.zeros(carry_ref.shape, carry_ref.dtype)


def _conv_mixer_kernel(*refs, groups, rows, tiles_per_seq, has_prefix):
    if has_prefix:
        x_ref, pre_ref, g_ref, win_ref, cw_ref, wout_ref, o_ref, st_ref, xn_scr, ext_scr, p_scr = refs
    else:
        x_ref, g_ref, win_ref, cw_ref, wout_ref, o_ref, st_ref, xn_scr, ext_scr, p_scr, carry_scr = refs
        _zero_at_sequence_start(carry_scr, tiles_per_seq)
    tm = groups * rows
    d = x_ref.shape[1]
    x = x_ref[...]
    xn_scr[...] = _rms(x, g_ref[...]).astype(BF16)
    for n in range(d // CHUNK):
        lo, hi = n * CHUNK, (n + 1) * CHUNK
        xn = xn_scr[...]
        b = _dot(xn, win_ref[:, lo:hi])
        c = _dot(xn, win_ref[:, d + lo:d + hi])
        h = _dot(xn, win_ref[:, 2 * d + lo:2 * d + hi])
        u3 = (c * h).reshape(groups, rows, CHUNK)
        if has_prefix:
            pre = pre_ref[:, :, lo:hi]
        else:
            pre = carry_scr[:, :, lo:hi]
        y3, new8 = _conv3(u3, pre, cw_ref[:, lo:hi], ext_scr)
        st_ref[:, :, lo:hi] = new8
        if not has_prefix:
            carry_scr[:, :, lo:hi] = new8
        p_scr[:, lo:hi] = (b * y3.reshape(tm, CHUNK)).astype(BF16)
    o_ref[...] = x + _dot(p_scr[...], wout_ref[...])


def _conv_mixer(x, state, g, w_in, cw, w_out, *, layer, seq_len):
    t, d = x.shape
    has_prefix = state is not None
    if has_prefix:
        groups, rows, tm, tps, n_state, st_rows = t // SUBLANES, SUBLANES, t, 1, t // SUBLANES, state.shape[2]
    else:
        tm = min(PROMPT_TILE, seq_len)
        groups, rows, tps, n_state, st_rows = 1, tm, seq_len // tm, t // seq_len, SUBLANES
    grid = (t // tm,)
    in_specs = [pl.BlockSpec((tm, d), lambda i: (i, 0))]
    args = [x]
    if has_prefix:
        in_specs.append(pl.BlockSpec((None, groups, st_rows, d), lambda i: (layer, 0, 0, 0)))
        args.append(state)
    in_specs += [_weight_spec(w, layer) for w in (g, w_in, cw, w_out)]
    args += [g, w_in, cw, w_out]
    scratch = [pltpu.VMEM((tm, d), BF16),
               pltpu.VMEM((groups, SUBLANES + rows, CHUNK), F32),
               pltpu.VMEM((tm, d), BF16)]
    if not has_prefix:
        scratch.append(pltpu.VMEM((1, SUBLANES, d), F32))
    st_block = (groups, st_rows, d)
    return pl.pallas_call(
        functools.partial(_conv_mixer_kernel, groups=groups, rows=rows, tiles_per_seq=tps,
                          has_prefix=has_prefix),
        grid=grid,
        in_specs=in_specs,
        out_specs=[pl.BlockSpec((tm, d), lambda i: (i, 0)),
                   pl.BlockSpec(st_block, lambda i: (i // tps, 0, 0))],
        out_shape=[jax.ShapeDtypeStruct((t, d), F32),
                   jax.ShapeDtypeStruct((n_state, st_rows, d), F32)],
        scratch_shapes=scratch,
        compiler_params=_params("arbitrary"),
        name="conv_mixer_sample" if has_prefix else "conv_mixer_prompt",
    )(*args)


def _ffn_kernel(*refs, groups, rows, tiles_per_seq, has_prefix, final, has_pending):
    refs = list(refs)
    x_ref = refs.pop(0)
    a_ref, wa_ref = (refs.pop(0), refs.pop(0)) if has_pending else (None, None)
    pre_ref = refs.pop(0) if has_prefix else None
    g_ref, wup_ref, cw_ref, wd_ref = refs[:4]
    refs = refs[4:]
    gf_ref = refs.pop(0) if final else None
    o_ref, st_ref, xn_scr, ext_scr, h_scr = refs[:5]
    carry_scr = None if has_prefix else refs[5]
    if not has_prefix:
        _zero_at_sequence_start(carry_scr, tiles_per_seq)
    tm = groups * rows
    f = wd_ref.shape[0]
    x = x_ref[...]
    if has_pending:
        x = x + _dot(a_ref[...].astype(BF16), wa_ref[...])
    xn_scr[...] = _rms(x, g_ref[...]).astype(BF16)
    for n in range(f // CHUNK):
        lo, hi = n * CHUNK, (n + 1) * CHUNK
        xn = xn_scr[...]
        ug = _dot(xn, wup_ref[:, lo:hi])
        uv = _dot(xn, wup_ref[:, f + lo:f + hi])
        if has_prefix:
            pre = pre_ref[:, :, lo:hi]
        else:
            pre = carry_scr[:, :, lo:hi]
        gc3, new8 = _conv3(ug.reshape(groups, rows, CHUNK), pre, cw_ref[:, lo:hi], ext_scr)
        st_ref[:, :, lo:hi] = new8
        if not has_prefix:
            carry_scr[:, :, lo:hi] = new8
        gc = gc3.reshape(tm, CHUNK)
        h_scr[:, lo:hi] = ((gc * jax.nn.sigmoid(gc)) * uv).astype(BF16)
    res = x + _dot(h_scr[...], wd_ref[...])
    o_ref[...] = _rms(res, gf_ref[...]) if final else res


def _conv_ffn(x, state, g, w_up, cw, wd, final_g, *, layer, seq_len, pending=None):
    t, d = x.shape
    f = wd.shape[1]
    has_prefix = state is not None
    final = final_g is not None
    if has_prefix:
        groups, rows, tm, tps, n_state, st_rows = t // SUBLANES, SUBLANES, t, 1, t // SUBLANES, state.shape[2]
    else:
        tm = min(PROMPT_TILE, seq_len)
        groups, rows, tps, n_state, st_rows = 1, tm, seq_len // tm, t // seq_len, SUBLANES
    in_specs = [pl.BlockSpec((tm, d), lambda i: (i, 0))]
    args = [x]
    if pending is not None:
        a, w_a = pending
        in_specs += [pl.BlockSpec((tm, a.shape[1]), lambda i: (i, 0)), _weight_spec(w_a, layer)]
        args += [a, w_a]
    if has_prefix:
        in_specs.append(pl.BlockSpec((None, groups, st_rows, f), lambda i: (layer, 0, 0, 0)))
        args.append(state)
    in_specs += [_weight_spec(w, layer) for w in (g, w_up, cw, wd)]
    args += [g, w_up, cw, wd]
    if final:
        in_specs.append(_const_spec((1, d)))
        args.append(final_g.reshape(1, d))
    scratch = [pltpu.VMEM((tm, d), BF16),
               pltpu.VMEM((groups, SUBLANES + rows, CHUNK), F32),
               pltpu.VMEM((tm, f), BF16)]
    if not has_prefix:
        scratch.append(pltpu.VMEM((1, SUBLANES, f), F32))
    return pl.pallas_call(
        functools.partial(_ffn_kernel, groups=groups, rows=rows, tiles_per_seq=tps,
                          has_prefix=has_prefix, final=final, has_pending=pending is not None),
        grid=(t // tm,),
        in_specs=in_specs,
        out_specs=[pl.BlockSpec((tm, d), lambda i: (i, 0)),
                   pl.BlockSpec((groups, st_rows, f), lambda i: (i // tps, 0, 0))],
        out_shape=[jax.ShapeDtypeStruct((t, d), F32),
                   jax.ShapeDtypeStruct((n_state, st_rows, f), F32)],
        scratch_shapes=scratch,
        compiler_params=_params("arbitrary"),
        name="conv_ffn_sample" if has_prefix else "conv_ffn_prompt",
    )(*args)


def _mla_proj_kernel(*refs, heads, prompt, q_scale):
    if prompt:
        (x_ref, g_ref, wd_ref, qg_ref, kvg_ref, wuq_ref, cos_ref, sin_ref, wuk_ref, wuvt_ref,
         qn_ref, qrope_ref, ckv_ref, kr_ref, kn_ref, vt_ref, krb_ref) = refs
    else:
        (x_ref, g_ref, wd_ref, qg_ref, kvg_ref, wuq_ref, cos_ref, sin_ref, wukt_ref,
         ql_ref, qr_ref, ckv_ref, kr_ref) = refs
    hn = heads * NOPE_DIM
    cos = cos_ref[...]
    sin = sin_ref[...]
    xn = _rms(x_ref[...], g_ref[...]).astype(BF16)
    dn = _dot(xn, wd_ref[...])
    c_q = _rms(dn[:, :Q_RANK], qg_ref[...]).astype(BF16)
    c_kv = _rms(dn[:, Q_RANK:Q_RANK + KV_RANK], kvg_ref[...])
    ckv_ref[...] = c_kv
    r0 = Q_RANK + KV_RANK
    k_r = dn[:, r0:r0 + LANES] * cos + dn[:, r0 + LANES:r0 + 2 * LANES] * sin
    kr_ref[...] = k_r
    q = _dot(c_q, wuq_ref[...])
    if prompt:
        hr = heads * ROPE_DIM
        qn_ref[...] = (q[:, :hn] * q_scale).astype(BF16)
        for c in range(hr // LANES):
            lo = hn + c * LANES
            qr_c = q[:, lo:lo + LANES] * cos + q[:, lo + hr:lo + hr + LANES] * sin
            qrope_ref[:, c * LANES:(c + 1) * LANES] = (qr_c * q_scale).astype(BF16)
    else:
        for h in range(heads):
            lo = hn + h * LANES
            qr_ref[h] = q[:, lo:lo + LANES] * cos + q[:, lo + heads * LANES:lo + (heads + 1) * LANES] * sin
            qn_h = q[:, h * NOPE_DIM:(h + 1) * NOPE_DIM].astype(BF16)
            ql_ref[h] = _dot(qn_h, wukt_ref[h])
    if prompt:
        ckv_b = c_kv.astype(BF16)
        kn_ref[...] = _dot(ckv_b, wuk_ref[...]).astype(BF16)
        vt_ref[...] = _dot_nt(wuvt_ref[...], ckv_b).astype(BF16)
        krb_ref[...] = k_r.astype(BF16)


def _mla_proj(x, g, wd_ext, qg, kvg, wuq_ext, cos, sin, extra, *, heads, seq_len, prompt, q_scale=None):
    t, d = x.shape
    tm = min(PROMPT_TILE, seq_len) if prompt else t
    tps = seq_len // tm if prompt else 1
    hn = heads * NOPE_DIM
    in_specs = [pl.BlockSpec((tm, d), lambda i: (i, 0)),
                _const_spec((1, d)), _const_spec(wd_ext.shape), _const_spec((1, Q_RANK)),
                _const_spec((1, KV_RANK)), _const_spec(wuq_ext.shape),
                pl.BlockSpec((tm, LANES), lambda i: (i % tps, 0)),
                pl.BlockSpec((tm, LANES), lambda i: (i % tps, 0))]
    in_specs += [_const_spec(w.shape) for w in extra]
    row = lambda w: pl.BlockSpec((tm, w), lambda i: (i, 0))
    if prompt:
        hr = heads * ROPE_DIM
        out_specs = [row(hn), row(hr), row(KV_RANK), row(LANES), row(hn),
                     pl.BlockSpec((hn, tm), lambda i: (0, i)), row(LANES)]
        out_shape = [jax.ShapeDtypeStruct((t, hn), BF16), jax.ShapeDtypeStruct((t, hr), BF16),
                     jax.ShapeDtypeStruct((t, KV_RANK), F32),
                     jax.ShapeDtypeStruct((t, LANES), F32), jax.ShapeDtypeStruct((t, hn), BF16),
                     jax.ShapeDtypeStruct((hn, t), BF16), jax.ShapeDtypeStruct((t, LANES), BF16)]
    else:
        head_major = lambda w: pl.BlockSpec((heads, tm, w), lambda i: (0, i, 0))
        out_specs = [head_major(KV_RANK), head_major(LANES), row(KV_RANK), row(LANES)]
        out_shape = [jax.ShapeDtypeStruct((heads, t, KV_RANK), F32), jax.ShapeDtypeStruct((heads, t, LANES), F32),
                     jax.ShapeDtypeStruct((t, KV_RANK), F32), jax.ShapeDtypeStruct((t, LANES), F32)]
    return pl.pallas_call(
        functools.partial(_mla_proj_kernel, heads=heads, prompt=prompt, q_scale=q_scale),
        grid=(t // tm,),
        in_specs=in_specs,
        out_specs=out_specs,
        out_shape=out_shape,
        compiler_params=_params("arbitrary"),
        name="mla_proj_prompt" if prompt else "mla_proj_sample",
    )(x, g.reshape(1, d), wd_ext, qg.reshape(1, Q_RANK), kvg.reshape(1, KV_RANK), wuq_ext, cos, sin, *extra)


def _attn_prompt_kernel(qn_ref, qr_ref, kn_ref, kr_ref, vt_ref, o_ref,
                        sa_scr, sb_scr, m_scr, acc_scr, *, t):
    i = pl.program_id(2)
    hps = sa_scr.shape[0]
    head = lambda g: slice(g * NOPE_DIM, (g + 1) * NOPE_DIM)
    qs = [jnp.concatenate([qn_ref[:, head(g)], qr_ref[:, g * ROPE_DIM:(g + 1) * ROPE_DIM]], axis=1)
          for g in range(hps)]
    m_scr[...] = jnp.full(m_scr.shape, -jnp.inf, F32)
    acc_scr[...] = jnp.zeros(acc_scr.shape, F32)
    ones = jnp.ones((BF16_SUBLANES, t), BF16)

    def scores(j, dst):
        ks = pl.multiple_of(j * t, t)
        kr = kr_ref[pl.ds(ks, t), :ROPE_DIM]
        for g in range(hps):
            k = jnp.concatenate([kn_ref[pl.ds(ks, t), head(g)], kr], axis=1)
            dst[g] = _dot_nt(k, qs[g])

    def update(src, j, diagonal):
        ks = pl.multiple_of(j * t, t)
        for g in range(hps):
            s = src[g]
            if diagonal:
                kpos = lax.broadcasted_iota(jnp.int32, (t, t), 0)
                qpos = lax.broadcasted_iota(jnp.int32, (t, t), 1)
                s = jnp.where(kpos <= qpos, s, -jnp.inf)
            m = m_scr[g]
            m_new = jnp.maximum(m, jnp.max(s, axis=0, keepdims=True))
            alpha = jnp.exp2(m - m_new)
            p = jnp.exp2(s - m_new).astype(BF16)
            v1 = jnp.concatenate([vt_ref[head(g), pl.ds(ks, t)], ones], axis=0)
            acc_scr[g] = alpha * acc_scr[g] + _dot(v1, p)
            m_scr[g] = m_new

    scores(0, sa_scr)

    def pair(jj, _):
        scores(2 * jj + 1, sb_scr)
        update(sa_scr, 2 * jj, False)
        scores(2 * jj + 2, sa_scr)
        update(sb_scr, 2 * jj + 1, False)
        return 0
    lax.fori_loop(0, i // 2, pair, 0)

    @pl.when(i % 2 == 0)
    def _():
        update(sa_scr, i, True)

    @pl.when(i % 2 == 1)
    def _():
        scores(i, sb_scr)
        update(sa_scr, i - 1, False)
        update(sb_scr, i, True)

    for g in range(hps):
        l = acc_scr[g, NOPE_DIM:NOPE_DIM + 1, :]
        o_ref[:, head(g)] = (acc_scr[g, :NOPE_DIM, :] * (1.0 / l)).T.astype(o_ref.dtype)


def _attn_prompt(qn, qrope, kn, krb, vt, *, heads, seq_len):
    tokens = qn.shape[0]
    bsz = tokens // seq_len
    t = min(ATTN_TILE, seq_len)
    nq = seq_len // t
    hps = ATTN_HEADS_PER_STEP
    hg = heads // hps
    w = hps * NOPE_DIM
    return pl.pallas_call(
        functools.partial(_attn_prompt_kernel, t=t),
        grid=(bsz, hg, nq),
        in_specs=[pl.BlockSpec((t, w), lambda b, h, i: (b * nq + i, h)),
                  pl.BlockSpec((t, hps * ROPE_DIM), lambda b, h, i: (b * nq + i, h)),
                  pl.BlockSpec((seq_len, w), lambda b, h, i: (b, h)),
                  pl.BlockSpec((seq_len, LANES), lambda b, h, i: (b, 0)),
                  pl.BlockSpec((w, seq_len), lambda b, h, i: (h, b))],
        out_specs=pl.BlockSpec((t, w), lambda b, h, i: (b * nq + i, h)),
        out_shape=jax.ShapeDtypeStruct((tokens, heads * NOPE_DIM), BF16),
        scratch_shapes=[pltpu.VMEM((hps, t, t), F32), pltpu.VMEM((hps, t, t), F32), pltpu.VMEM((hps, 1, t), F32),
                        pltpu.VMEM((hps, NOPE_DIM + BF16_SUBLANES, t), F32)],
        compiler_params=_params("arbitrary", "arbitrary", "arbitrary"),
        name="mla_attn_prompt",
    )(qn, qrope, kn, krb, vt)


def _attn_sample_kernel(pt_ref, ql_ref, qr_ref, cn_ref, kn_ref, lat_hbm, ropet_hbm, o_ref,
                        lat_buf, rope_buf, s_scr, sem, *, layer, n_pages, page, ppc, ts, scale):
    b = pl.program_id(0)
    nb = pl.num_programs(0)
    slot = b % 2
    heads = ql_ref.shape[0]
    spp = s_scr.shape[0]
    tk = ppc * page
    n_chunks = n_pages // ppc
    seq_rows = lambda k: slice(k * ts, (k + 1) * ts)

    def start_fetch(group, dst_slot):
        for k in range(spp):
            for p in range(n_pages):
                pg = pt_ref[group * spp + k, p]
                pltpu.make_async_copy(lat_hbm.at[layer, pg], lat_buf.at[dst_slot, k, p * page:(p + 1) * page, :],
                                      sem.at[0, dst_slot]).start()
                pltpu.make_async_copy(ropet_hbm.at[layer, pg], rope_buf.at[dst_slot, k, p],
                                      sem.at[1, dst_slot]).start()

    def wait_fetch(dst_slot):
        pltpu.make_async_copy(lat_buf.at[dst_slot], lat_buf.at[dst_slot], sem.at[0, dst_slot]).wait()
        pltpu.make_async_copy(rope_buf.at[dst_slot], rope_buf.at[dst_slot], sem.at[1, dst_slot]).wait()

    @pl.when(b == 0)
    def _():
        start_fetch(0, 0)

    wait_fetch(slot)
    start_fetch((b + 1) % nb, 1 - slot)

    qls = [ql_ref[:, seq_rows(k), :].reshape(heads * ts, KV_RANK).astype(BF16) for k in range(spp)]
    qrs = [qr_ref[:, seq_rows(k), :].reshape(heads * ts, LANES)[:, :ROPE_DIM].astype(BF16) for k in range(spp)]

    def lat_chunk(k, c):
        return lat_buf[slot, k, c * tk:(c + 1) * tk, :]

    for c in range(n_chunks):
        for k in range(spp):
            ropt = jnp.concatenate([rope_buf[slot, k, c * ppc + j] for j in range(ppc)], axis=1).astype(BF16)
            lat_t = lat_chunk(k, c).T.astype(BF16)
            s_scr[k, :, c * tk:(c + 1) * tk] = (_dot(qls[k], lat_t) + _dot(qrs[k], ropt)) * scale

    pad = BF16_SUBLANES - ts
    ms, ls, accs = [], [], []
    for k in range(spp):
        cn = jnp.concatenate([cn_ref[seq_rows(k), :], jnp.zeros((pad, KV_RANK), F32)], axis=0).astype(BF16)
        kn = jnp.concatenate([kn_ref[seq_rows(k), :][:, :ROPE_DIM], jnp.zeros((pad, ROPE_DIM), F32)],
                             axis=0).astype(BF16)
        s_new = (_dot_nt(qls[k], cn) + _dot_nt(qrs[k], kn)) * scale
        t_q = lax.broadcasted_iota(jnp.int32, s_new.shape, 0) % ts
        t_k = lax.broadcasted_iota(jnp.int32, s_new.shape, 1)
        s_new = jnp.where(t_k <= t_q, s_new, -jnp.inf)
        m = jnp.maximum(jnp.max(s_scr[k], axis=-1, keepdims=True), jnp.max(s_new, axis=-1, keepdims=True))
        p_new = jnp.exp(s_new - m)
        ms.append(m)
        ls.append(jnp.sum(p_new, axis=-1, keepdims=True))
        accs.append(_dot(p_new.astype(BF16), cn))
    for c in range(n_chunks):
        for k in range(spp):
            p = jnp.exp(s_scr[k, :, c * tk:(c + 1) * tk] - ms[k])
            ls[k] = ls[k] + jnp.sum(p, axis=-1, keepdims=True)
            accs[k] = accs[k] + _dot(p.astype(BF16), lat_chunk(k, c).astype(BF16))
    for k in range(spp):
        o_ref[:, seq_rows(k), :] = (accs[k] * (1.0 / ls[k])).reshape(heads, ts, KV_RANK)

    @pl.when(b == nb - 1)
    def _():
        wait_fetch(1 - slot)


def _attn_sample(page_table, ql, qr, ckv, kr, cache_lat, cache_rope_t, *, layer, ts, scale):
    heads, t, _ = ql.shape
    n_seq, n_pages = page_table.shape
    page = cache_lat.shape[2]
    n_past = n_pages * page
    ppc = min(PAST_CHUNK_PAGES, n_pages)
    spp = min(PAST_SEQS_PER_STEP, n_seq)
    rows = spp * ts
    grid_spec = pltpu.PrefetchScalarGridSpec(
        num_scalar_prefetch=1,
        grid=(n_seq // spp,),
        in_specs=[pl.BlockSpec((heads, rows, KV_RANK), lambda b, pt: (0, b, 0)),
                  pl.BlockSpec((heads, rows, LANES), lambda b, pt: (0, b, 0)),
                  pl.BlockSpec((rows, KV_RANK), lambda b, pt: (b, 0)),
                  pl.BlockSpec((rows, LANES), lambda b, pt: (b, 0)),
                  pl.BlockSpec(memory_space=pl.ANY),
                  pl.BlockSpec(memory_space=pl.ANY)],
        out_specs=pl.BlockSpec((heads, rows, KV_RANK), lambda b, pt: (0, b, 0)),
        scratch_shapes=[pltpu.VMEM((2, spp, n_past, KV_RANK), F32),
                        pltpu.VMEM((2, spp, n_pages, ROPE_DIM, page), F32),
                        pltpu.VMEM((spp, heads * ts, n_past), F32),
                        pltpu.SemaphoreType.DMA((2, 2))],
    )
    return pl.pallas_call(
        functools.partial(_attn_sample_kernel, layer=layer, n_pages=n_pages, page=page, ppc=ppc, ts=ts,
                          scale=scale),
        grid_spec=grid_spec,
        out_shape=jax.ShapeDtypeStruct((heads, t, KV_RANK), F32),
        compiler_params=_params("arbitrary"),
        name="mla_attn_sample",
    )(page_table, ql, qr, ckv, kr, cache_lat, cache_rope_t)


def _mla_sample_out_kernel(ol_ref, wuv_ref, wo_ref, x_ref, o_ref, os_scr):
    heads = ol_ref.shape[0]
    vd = wuv_ref.shape[2]
    for h in range(heads):
        os_scr[:, h * vd:(h + 1) * vd] = _dot(ol_ref[h].astype(BF16), wuv_ref[h]).astype(BF16)
    o_ref[...] = x_ref[...] + _dot(os_scr[...], wo_ref[...])


def _mla_sample_out(ol, wuv_h, wo, x):
    t, d = x.shape
    heads, _, vd = wuv_h.shape
    return pl.pallas_call(
        _mla_sample_out_kernel,
        grid=(1,),
        in_specs=[_const_spec(ol.shape), _const_spec(wuv_h.shape), _const_spec(wo.shape), _const_spec(x.shape)],
        out_specs=pl.BlockSpec((t, d), lambda i: (0, 0)),
        out_shape=jax.ShapeDtypeStruct((t, d), F32),
        scratch_shapes=[pltpu.VMEM((t, heads * vd), BF16)],
        compiler_params=_params("arbitrary"),
        name="mla_out_sample",
    )(ol, wuv_h, wo, x)


def _mem_kv_kernel(mem_ref, g_ref, wk_ref, wv_ref, k_ref, v_ref, mn_scr):
    @pl.when(pl.program_id(1) == 0)
    def _():
        mn_scr[...] = _rms(mem_ref[...], g_ref[...]).astype(BF16)
    mn = mn_scr[...]
    k_ref[...] = _dot(mn, wk_ref[...])
    v_ref[...] = _dot(mn, wv_ref[...])


def _mem_kv(mem, g, wkv, *, tn=512):
    m, d = mem.shape
    depth = wkv.shape[0]
    e = wkv.shape[2] // 2
    nk = e // tn
    return pl.pallas_call(
        _mem_kv_kernel,
        grid=(depth, nk),
        in_specs=[_const_spec((m, d)),
                  pl.BlockSpec((None, 1, d), lambda l, n: (l, 0, 0)),
                  pl.BlockSpec((None, d, tn), lambda l, n: (l, 0, n)),
                  pl.BlockSpec((None, d, tn), lambda l, n: (l, 0, nk + n))],
        out_specs=[pl.BlockSpec((None, m, tn), lambda l, n: (l, 0, n)),
                   pl.BlockSpec((None, m, tn), lambda l, n: (l, 0, n))],
        out_shape=[jax.ShapeDtypeStruct((depth, m, e), F32), jax.ShapeDtypeStruct((depth, m, e), F32)],
        scratch_shapes=[pltpu.VMEM((m, d), BF16)],
        compiler_params=_params("arbitrary", "arbitrary"),
        name="mem_kv",
    )(mem, g, wkv, wkv)


def _xattn_heads(q, k_of, v_of, heads, scale):
    e = q.shape[1] // heads
    outs = []
    for h in range(heads):
        s = _dot_nt(q[:, h * e:(h + 1) * e].astype(BF16), k_of(h)) * scale
        p = jnp.exp(s - jnp.max(s, axis=-1, keepdims=True))
        p = p * (1.0 / jnp.sum(p, axis=-1, keepdims=True))
        outs.append(_dot(p.astype(BF16), v_of(h)))
    return outs


def _xattn_prompt_kernel(*refs, heads, scale, has_mixer_out):
    if has_mixer_out:
        x_ref, a_ref, wa_ref, g_ref, wq_ref, mk_ref, mv_ref, wo_ref, o_ref, o_scr = refs
        x = x_ref[...] + _dot(a_ref[...], wa_ref[...])
    else:
        x_ref, g_ref, wq_ref, mk_ref, mv_ref, wo_ref, o_ref, o_scr = refs
        x = x_ref[...]
    q = _dot(_rms(x, g_ref[...]).astype(BF16), wq_ref[...])
    e = q.shape[1] // heads
    outs = _xattn_heads(q, lambda h: mk_ref[:, h * e:(h + 1) * e].astype(BF16),
                        lambda h: mv_ref[:, h * e:(h + 1) * e].astype(BF16), heads, scale)
    for h in range(heads):
        o_scr[:, h * e:(h + 1) * e] = outs[h].astype(BF16)
    o_ref[...] = x + _dot(o_scr[...], wo_ref[...])


def _xattn_prompt(x, g, wq, mk_all, mv_all, wo, *, layer, heads, seq_len, n_mem, scale, mixer_out=None):
    t, d = x.shape
    tm = min(XATTN_TILE, seq_len)
    tps = seq_len // tm
    row = lambda w: pl.BlockSpec((tm, w), lambda i: (i, 0))
    in_specs, args = [row(d)], [x]
    if mixer_out is not None:
        a, w_a = mixer_out
        in_specs += [row(a.shape[1]), _weight_spec(w_a)]
        args += [a, w_a]
    in_specs += [_weight_spec(g, layer), _weight_spec(wq, layer),
                 pl.BlockSpec((None, n_mem, d), lambda i: (layer, i // tps, 0)),
                 pl.BlockSpec((None, n_mem, d), lambda i: (layer, i // tps, 0)),
                 _weight_spec(wo, layer)]
    args += [g, wq, mk_all, mv_all, wo]
    return pl.pallas_call(
        functools.partial(_xattn_prompt_kernel, heads=heads, scale=scale, has_mixer_out=mixer_out is not None),
        grid=(t // tm,),
        in_specs=in_specs,
        out_specs=row(d),
        out_shape=jax.ShapeDtypeStruct((t, d), F32),
        scratch_shapes=[pltpu.VMEM((tm, d), BF16)],
        compiler_params=_params("arbitrary"),
        name="xattn_prompt",
    )(*args)


def _norm_matmul_kernel(x_ref, g_ref, w_ref, o_ref):
    o_ref[...] = _dot(_rms(x_ref[...], g_ref[...]).astype(BF16), w_ref[...])


def _norm_matmul(x, g, w, *, layer):
    t, d = x.shape
    n = w.shape[-1]
    return pl.pallas_call(
        _norm_matmul_kernel,
        grid=(1,),
        in_specs=[_const_spec((t, d)), _weight_spec(g, layer), _weight_spec(w, layer)],
        out_specs=pl.BlockSpec((t, n), lambda i: (0, 0)),
        out_shape=jax.ShapeDtypeStruct((t, n), F32),
        compiler_params=_params("arbitrary"),
        name="norm_matmul",
    )(x, g, w)


def _xattn_sample_kernel(q_ref, mk_ref, mv_ref, o_ref, *, ts, scale):
    nseq, _, heads, e = mk_ref.shape
    for s_i in range(nseq):
        q = q_ref[s_i * ts:(s_i + 1) * ts, :]
        q_hm = jnp.concatenate([q[:, h * e:(h + 1) * e] for h in range(heads)], axis=0).astype(BF16)
        k_all = mk_ref[s_i].reshape(-1, e).astype(BF16)
        v_all = mv_ref[s_i].reshape(-1, e).astype(BF16)
        s = _dot_nt(q_hm, k_all) * scale
        key_head = lax.broadcasted_iota(jnp.int32, s.shape, 1) % heads
        row_head = lax.broadcasted_iota(jnp.int32, s.shape, 0) // ts
        s = jnp.where(key_head == row_head, s, -jnp.inf)
        p = jnp.exp(s - jnp.max(s, axis=-1, keepdims=True))
        p = p * (1.0 / jnp.sum(p, axis=-1, keepdims=True))
        o = _dot(p.astype(BF16), v_all)
        for h in range(heads):
            o_ref[s_i * ts:(s_i + 1) * ts, h * e:(h + 1) * e] = o[h * ts:(h + 1) * ts]


def _xattn_sample(q, mk_all, mv_all, *, layer, ts, scale, seqs_per_step=8):
    t, d = q.shape
    _, n_seq, n_mem, heads, e = mk_all.shape
    sp = min(seqs_per_step, n_seq)
    cache_spec = pl.BlockSpec((None, sp, n_mem, heads, e), lambda i: (layer, i, 0, 0, 0))
    return pl.pallas_call(
        functools.partial(_xattn_sample_kernel, ts=ts, scale=scale),
        grid=(n_seq // sp,),
        in_specs=[pl.BlockSpec((sp * ts, d), lambda i: (i, 0)), cache_spec, cache_spec],
        out_specs=pl.BlockSpec((sp * ts, d), lambda i: (i, 0)),
        out_shape=jax.ShapeDtypeStruct((t, d), F32),
        compiler_params=_params("arbitrary"),
        name="xattn_sample",
    )(q, mk_all, mv_all)


def _rope_tables(pos):
    half = ROPE_DIM // 2
    inv = 1.0 / (ROPE_THETA ** (jnp.arange(half, dtype=F32) * (2.0 / ROPE_DIM)))
    ang = pos.astype(F32)[:, None] * inv[None, :]
    reps = LANES // half
    return jnp.tile(jnp.cos(ang), (1, reps)), jnp.tile(jnp.sin(ang), (1, reps))


def _rot_cols(w):
    half = w.shape[-1] // 2
    return jnp.concatenate([-w[..., half:], w[..., :half]], axis=-1)


def _pad_lanes(w):
    return jnp.pad(w, [(0, 0)] * (w.ndim - 1) + [(0, LANES - w.shape[-1])])


def kernel(x_prompt, x_sample, state_conv, cache_kv_latent, cache_k_rope, state_ffn_conv, cache_mem_k, cache_mem_v, page_table, mem_prompt, conv_norm_g, conv_w_in, conv_w, conv_w_out, mla_norm_g, mla_w_down, mla_q_norm_g, mla_kv_norm_g, mla_w_uq, mla_w_uk, mla_w_uv, mla_w_o, xa_norm_g, xa_mem_norm_g, xa_w_q, xa_w_kv, xa_w_o, ffn_norm_g, ffn_w_up, ffn_conv_w, ffn_w_down, final_norm_g):
    bp, sp, d = x_prompt.shape
    bs, ts, _ = x_sample.shape
    depth = ffn_w_up.shape[0]
    heads = mla_w_uq.shape[2]
    xa_heads, xa_dim = xa_w_q.shape[2], xa_w_q.shape[3]
    n_mem = mem_prompt.shape[1]
    n_pages, page = page_table.shape[1], cache_kv_latent.shape[2]
    score_scale = (NOPE_DIM + ROPE_DIM) ** -0.5
    xa_scale = xa_dim ** -0.5
    assert ts == SUBLANES and KV_RANK + Q_RANK + ROPE_DIM == mla_w_down.shape[2]

    xp = x_prompt.reshape(bp * sp, d)
    xs = x_sample.reshape(bs * ts, d)

    cos_p, sin_p = _rope_tables(jnp.arange(sp, dtype=jnp.int32))
    cos_s, sin_s = _rope_tables(n_pages * page + jnp.arange(ts, dtype=jnp.int32))
    cos_s, sin_s = jnp.tile(cos_s, (bs, 1)), jnp.tile(sin_s, (bs, 1))

    xa_e = xa_heads * xa_dim
    gain = lambda g: g.reshape(g.shape[0], 1, g.shape[1])
    conv_g, xa_g, ffn_g = gain(conv_norm_g), gain(xa_norm_g), gain(ffn_norm_g)
    w_in_all, w_out_all = conv_w_in.astype(BF16), conv_w_out.astype(BF16)
    wq_all = xa_w_q.reshape(depth, d, xa_e).astype(BF16)
    wo_x_all = xa_w_o.reshape(depth, xa_e, d).astype(BF16)
    w_up_all, w_dn_all = ffn_w_up.astype(BF16), ffn_w_down.astype(BF16)

    wkv_all = xa_w_kv.reshape(depth, d, 2 * xa_e).astype(BF16)
    mk_all, mv_all = _mem_kv(mem_prompt.reshape(bp * n_mem, d), gain(xa_mem_norm_g), wkv_all)
    cache_rope_t = jnp.swapaxes(cache_k_rope, 2, 3)

    conv_p, conv_s, lat_p, rop_p, lat_s, rop_s, ffn_p, ffn_s = [], [], [], [], [], [], [], []

    for i in range(depth):
        j = i // N_MIXERS
        prompt_mixer_out = None
        if i % N_MIXERS == 0:
            xp, st = _conv_mixer(xp, None, conv_g, w_in_all, conv_w, w_out_all, layer=j, seq_len=sp)
            conv_p.append(st[:, SUBLANES - 2:, :])
            xs, st = _conv_mixer(xs, state_conv, conv_g, w_in_all, conv_w, w_out_all, layer=j, seq_len=ts)
            conv_s.append(st)
        else:
            wd = mla_w_down[j]
            w_r = wd[:, Q_RANK + KV_RANK:]
            wd_ext = jnp.concatenate([wd[:, :Q_RANK + KV_RANK], _pad_lanes(w_r), _pad_lanes(_rot_cols(w_r))],
                                     axis=1).astype(BF16)
            wuq = mla_w_uq[j]
            wuq_r = wuq[:, :, NOPE_DIM:]
            wuq_n = wuq[:, :, :NOPE_DIM].reshape(Q_RANK, heads * NOPE_DIM)
            wuq_ext = jnp.concatenate([wuq_n, _pad_lanes(wuq_r).reshape(Q_RANK, heads * LANES),
                                       _pad_lanes(_rot_cols(wuq_r)).reshape(Q_RANK, heads * LANES)],
                                      axis=1).astype(BF16)
            wuq_packed = jnp.concatenate([wuq_n, wuq_r.reshape(Q_RANK, heads * ROPE_DIM),
                                          _rot_cols(wuq_r).reshape(Q_RANK, heads * ROPE_DIM)], axis=1).astype(BF16)
            wuk = mla_w_uk[j].astype(BF16)
            wuv = mla_w_uv[j].astype(BF16)
            wo = mla_w_o[j].reshape(heads * NOPE_DIM, d).astype(BF16)

            qn, qrope, ckv, kr, kn, vt, krb = _mla_proj(
                xp, mla_norm_g[j], wd_ext, mla_q_norm_g[j], mla_kv_norm_g[j], wuq_packed, cos_p, sin_p,
                [wuk.reshape(KV_RANK, heads * NOPE_DIM), wuv.reshape(KV_RANK, heads * NOPE_DIM).T],
                heads=heads, seq_len=sp, prompt=True, q_scale=score_scale * LOG2_E)
            op = _attn_prompt(qn, qrope, kn, krb, vt, heads=heads, seq_len=sp)
            prompt_mixer_out = (op, wo)
            lat_p.append(ckv.reshape(bp, sp, KV_RANK))
            rop_p.append(kr[:, :ROPE_DIM].reshape(bp, sp, ROPE_DIM))

            ql, qr, ckv_s, kr_s = _mla_proj(
                xs, mla_norm_g[j], wd_ext, mla_q_norm_g[j], mla_kv_norm_g[j], wuq_ext, cos_s, sin_s,
                [wuk.transpose(1, 2, 0)], heads=heads, seq_len=ts, prompt=False)
            ol = _attn_sample(page_table, ql, qr, ckv_s, kr_s, cache_kv_latent, cache_rope_t,
                              layer=j, ts=ts, scale=score_scale)
            xs = _mla_sample_out(ol, wuv.transpose(1, 0, 2), wo, xs)
            lat_s.append(ckv_s.reshape(bs, ts, KV_RANK))
            rop_s.append(kr_s[:, :ROPE_DIM].reshape(bs, ts, ROPE_DIM))

        xp = _xattn_prompt(xp, xa_g, wq_all, mk_all, mv_all, wo_x_all, layer=i, heads=xa_heads, seq_len=sp,
                           n_mem=n_mem, scale=xa_scale, mixer_out=prompt_mixer_out)
        qs = _norm_matmul(xs, xa_g, wq_all, layer=i)
        os_ = _xattn_sample(qs, cache_mem_k, cache_mem_v, layer=i, ts=ts, scale=xa_scale)

        fg = final_norm_g if i == depth - 1 else None
        xp, st = _conv_ffn(xp, None, ffn_g, w_up_all, ffn_conv_w, w_dn_all, fg, layer=i, seq_len=sp)
        ffn_p.append(st[:, SUBLANES - 2:, :])
        xs, st = _conv_ffn(xs, state_ffn_conv, ffn_g, w_up_all, ffn_conv_w, w_dn_all, fg, layer=i, seq_len=ts,
                           pending=(os_, wo_x_all))
        ffn_s.append(st)

    mem_shape = (depth, bp, n_mem, xa_heads, xa_dim)
    return (xp.reshape(bp, sp, d), xs.reshape(bs, ts, d),
            jnp.stack(conv_p), jnp.stack(conv_s),
            jnp.stack(lat_p), jnp.stack(rop_p), jnp.stack(lat_s), jnp.stack(rop_s),
            jnp.stack(ffn_p), jnp.stack(ffn_s),
            mk_all.reshape(mem_shape), mv_all.reshape(mem_shape))
```

```python
import functools
import math

import jax
import jax.numpy as jnp
from jax import lax
from jax.experimental import pallas as pl
from jax.experimental.pallas import tpu as pltpu

F32 = jnp.float32
BF16 = jnp.bfloat16

RMS_EPS = 1e-6
ROPE_THETA = 10000.0
N_MIXERS = 2
NOPE_DIM = 128
ROPE_DIM = 64
Q_RANK = 384
KV_RANK = 256

SUBLANES = 8
BF16_SUBLANES = 16
LANES = 128
MXU_WIDTH = 256
VMEM_LIMIT_BYTES = 56 * 1024 * 1024

PROMPT_TILE = 512
XATTN_TILE = 1024
CHUNK = MXU_WIDTH
ATTN_TILE = 512
ATTN_HEADS_PER_STEP = 4
PAST_SEQS_PER_STEP = 2
PAST_CHUNK_PAGES = 8
LOG2_E = math.log2(math.e)


def _dot(a, b):
    return jnp.dot(a, b, preferred_element_type=F32)


def _dot_nt(a, b):
    return lax.dot_general(a, b, (((1,), (1,)), ((), ())), preferred_element_type=F32)


def _rms(xf, g):
    y = xf * lax.rsqrt(jnp.mean(xf * xf, axis=-1, keepdims=True) + RMS_EPS)
    return y * g


def _params(*sem):
    return pltpu.CompilerParams(dimension_semantics=tuple(sem), vmem_limit_bytes=VMEM_LIMIT_BYTES)


def _const_spec(shape):
    nd = len(shape)
    return pl.BlockSpec(shape, lambda *_: (0,) * nd, pipeline_mode=pl.Buffered(1))


def _weight_spec(w, layer=None):
    if layer is None:
        return _const_spec(w.shape)
    nd = w.ndim - 1
    return pl.BlockSpec((None,) + w.shape[1:], lambda *_: (layer,) + (0,) * nd, pipeline_mode=pl.Buffered(1))


def _conv3(u3, prefix, w, ext_ref):
    r, p = u3.shape[1], prefix.shape[1]
    ext_ref[:, SUBLANES - p:SUBLANES, :] = prefix
    ext_ref[:, SUBLANES:SUBLANES + r, :] = u3
    y = (ext_ref[:, SUBLANES - 2:SUBLANES - 2 + r, :] * w[0:1, :][None]
         + ext_ref[:, SUBLANES - 1:SUBLANES - 1 + r, :] * w[1:2, :][None]
         + u3 * w[2:3, :][None])
    return y, ext_ref[:, r + SUBLANES - p:r + SUBLANES, :]


def _zero_at_sequence_start(carry_ref, tiles_per_seq):
    @pl.when(pl.program_id(0) % tiles_per_seq == 0)
    def _():
        carry_ref[...] = jnp.zeros(carry_ref.shape, carry_ref.dtype)


def _conv_mixer_kernel(*refs, groups, rows, tiles_per_seq, has_prefix):
    if has_prefix:
        x_ref, pre_ref, g_ref, win_ref, cw_ref, wout_ref, o_ref, st_ref, xn_scr, ext_scr, p_scr = refs
    else:
        x_ref, g_ref, win_ref, cw_ref, wout_ref, o_ref, st_ref, xn_scr, ext_scr, p_scr, carry_scr = refs
        _zero_at_sequence_start(carry_scr, tiles_per_seq)
    tm = groups * rows
    d = x_ref.shape[1]
    x = x_ref[...]
    xn_scr[...] = _rms(x, g_ref[...]).astype(BF16)
    for n in range(d // CHUNK):
        lo, hi = n * CHUNK, (n + 1) * CHUNK
        xn = xn_scr[...]
        b = _dot(xn, win_ref[:, lo:hi])
        c = _dot(xn, win_ref[:, d + lo:d + hi])
        h = _dot(xn, win_ref[:, 2 * d + lo:2 * d + hi])
        u3 = (c * h).reshape(groups, rows, CHUNK)
        if has_prefix:
            pre = pre_ref[:, :, lo:hi]
        else:
            pre = carry_scr[:, :, lo:hi]
        y3, new8 = _conv3(u3, pre, cw_ref[:, lo:hi], ext_scr)
        st_ref[:, :, lo:hi] = new8
        if not has_prefix:
            carry_scr[:, :, lo:hi] = new8
        p_scr[:, lo:hi] = (b * y3.reshape(tm, CHUNK)).astype(BF16)
    o_ref[...] = x + _dot(p_scr[...], wout_ref[...])


def _conv_mixer(x, state, g, w_in, cw, w_out, *, layer, seq_len):
    t, d = x.shape
    has_prefix = state is not None
    if has_prefix:
        groups, rows, tm, tps, n_state, st_rows = t // SUBLANES, SUBLANES, t, 1, t // SUBLANES, state.shape[2]
    else:
        tm = min(PROMPT_TILE, seq_len)
        groups, rows, tps, n_state, st_rows = 1, tm, seq_len // tm, t // seq_len, SUBLANES
    grid = (t // tm,)
    in_specs = [pl.BlockSpec((tm, d), lambda i: (i, 0))]
    args = [x]
    if has_prefix:
        in_specs.append(pl.BlockSpec((None, groups, st_rows, d), lambda i: (layer, 0, 0, 0)))
        args.append(state)
    in_specs += [_weight_spec(w, layer) for w in (g, w_in, cw, w_out)]
    args += [g, w_in, cw, w_out]
    scratch = [pltpu.VMEM((tm, d), BF16),
               pltpu.VMEM((groups, SUBLANES + rows, CHUNK), F32),
               pltpu.VMEM((tm, d), BF16)]
    if not has_prefix:
        scratch.append(pltpu.VMEM((1, SUBLANES, d), F32))
    st_block = (groups, st_rows, d)
    return pl.pallas_call(
        functools.partial(_conv_mixer_kernel, groups=groups, rows=rows, tiles_per_seq=tps,
                          has_prefix=has_prefix),
        grid=grid,
        in_specs=in_specs,
        out_specs=[pl.BlockSpec((tm, d), lambda i: (i, 0)),
                   pl.BlockSpec(st_block, lambda i: (i // tps, 0, 0))],
        out_shape=[jax.ShapeDtypeStruct((t, d), F32),
                   jax.ShapeDtypeStruct((n_state, st_rows, d), F32)],
        scratch_shapes=scratch,
        compiler_params=_params("arbitrary"),
        name="conv_mixer_sample" if has_prefix else "conv_mixer_prompt",
    )(*args)


def _ffn_kernel(*refs, groups, rows, tiles_per_seq, has_prefix, final, has_pending):
    refs = list(refs)
    x_ref = refs.pop(0)
    a_ref, wa_ref = (refs.pop(0), refs.pop(0)) if has_pending else (None, None)
    pre_ref = refs.pop(0) if has_prefix else None
    g_ref, wup_ref, cw_ref, wd_ref = refs[:4]
    refs = refs[4:]
    gf_ref = refs.pop(0) if final else None
    o_ref, st_ref, xn_scr, ext_scr, h_scr = refs[:5]
    carry_scr = None if has_prefix else refs[5]
    if not has_prefix:
        _zero_at_sequence_start(carry_scr, tiles_per_seq)
    tm = groups * rows
    f = wd_ref.shape[0]
    x = x_ref[...]
    if has_pending:
        x = x + _dot(a_ref[...].astype(BF16), wa_ref[...])
    xn_scr[...] = _rms(x, g_ref[...]).astype(BF16)
    for n in range(f // CHUNK):
        lo, hi = n * CHUNK, (n + 1) * CHUNK
        xn = xn_scr[...]
        ug = _dot(xn, wup_ref[:, lo:hi])
        uv = _dot(xn, wup_ref[:, f + lo:f + hi])
        if has_prefix:
            pre = pre_ref[:, :, lo:hi]
        else:
            pre = carry_scr[:, :, lo:hi]
        gc3, new8 = _conv3(ug.reshape(groups, rows, CHUNK), pre, cw_ref[:, lo:hi], ext_scr)
        st_ref[:, :, lo:hi] = new8
        if not has_prefix:
            carry_scr[:, :, lo:hi] = new8
        gc = gc3.reshape(tm, CHUNK)
        h_scr[:, lo:hi] = ((gc * jax.nn.sigmoid(gc)) * uv).astype(BF16)
    res = x + _dot(h_scr[...], wd_ref[...])
    o_ref[...] = _rms(res, gf_ref[...]) if final else res


def _conv_ffn(x, state, g, w_up, cw, wd, final_g, *, layer, seq_len, pending=None):
    t, d = x.shape
    f = wd.shape[1]
    has_prefix = state is not None
    final = final_g is not None
    if has_prefix:
        groups, rows, tm, tps, n_state, st_rows = t // SUBLANES, SUBLANES, t, 1, t // SUBLANES, state.shape[2]
    else:
        tm = min(PROMPT_TILE, seq_len)
        groups, rows, tps, n_state, st_rows = 1, tm, seq_len // tm, t // seq_len, SUBLANES
    in_specs = [pl.BlockSpec((tm, d), lambda i: (i, 0))]
    args = [x]
    if pending is not None:
        a, w_a = pending
        in_specs += [pl.BlockSpec((tm, a.shape[1]), lambda i: (i, 0)), _weight_spec(w_a, layer)]
        args += [a, w_a]
    if has_prefix:
        in_specs.append(pl.BlockSpec((None, groups, st_rows, f), lambda i: (layer, 0, 0, 0)))
        args.append(state)
    in_specs += [_weight_spec(w, layer) for w in (g, w_up, cw, wd)]
    args += [g, w_up, cw, wd]
    if final:
        in_specs.append(_const_spec((1, d)))
        args.append(final_g.reshape(1, d))
    scratch = [pltpu.VMEM((tm, d), BF16),
               pltpu.VMEM((groups, SUBLANES + rows, CHUNK), F32),
               pltpu.VMEM((tm, f), BF16)]
    if not has_prefix:
        scratch.append(pltpu.VMEM((1, SUBLANES, f), F32))
    return pl.pallas_call(
        functools.partial(_ffn_kernel, groups=groups, rows=rows, tiles_per_seq=tps,
                          has_prefix=has_prefix, final=final, has_pending=pending is not None),
        grid=(t // tm,),
        in_specs=in_specs,
        out_specs=[pl.BlockSpec((tm, d), lambda i: (i, 0)),
                   pl.BlockSpec((groups, st_rows, f), lambda i: (i // tps, 0, 0))],
        out_shape=[jax.ShapeDtypeStruct((t, d), F32),
                   jax.ShapeDtypeStruct((n_state, st_rows, f), F32)],
        scratch_shapes=scratch,
        compiler_params=_params("arbitrary"),
        name="conv_ffn_sample" if has_prefix else "conv_ffn_prompt",
    )(*args)


def _mla_proj_kernel(*refs, heads, prompt, q_scale):
    if prompt:
        (x_ref, g_ref, wd_ref, qg_ref, kvg_ref, wuq_ref, cos_ref, sin_ref, wuk_ref, wuvt_ref,
         qn_ref, qrope_ref, ckv_ref, kr_ref, kn_ref, vt_ref, krb_ref) = refs
    else:
        (x_ref, g_ref, wd_ref, qg_ref, kvg_ref, wuq_ref, cos_ref, sin_ref, wukt_ref,
         ql_ref, qr_ref, ckv_ref, kr_ref) = refs
    hn = heads * NOPE_DIM
    cos = cos_ref[...]
    sin = sin_ref[...]
    xn = _rms(x_ref[...], g_ref[...]).astype(BF16)
    dn = _dot(xn, wd_ref[...])
    c_q = _rms(dn[:, :Q_RANK], qg_ref[...]).astype(BF16)
    c_kv = _rms(dn[:, Q_RANK:Q_RANK + KV_RANK], kvg_ref[...])
    ckv_ref[...] = c_kv
    r0 = Q_RANK + KV_RANK
    k_r = dn[:, r0:r0 + LANES] * cos + dn[:, r0 + LANES:r0 + 2 * LANES] * sin
    kr_ref[...] = k_r
    q = _dot(c_q, wuq_ref[...])
    if prompt:
        hr = heads * ROPE_DIM
        qn_ref[...] = (q[:, :hn] * q_scale).astype(BF16)
        for c in range(hr // LANES):
            lo = hn + c * LANES
            qr_c = q[:, lo:lo + LANES] * cos + q[:, lo + hr:lo + hr + LANES] * sin
            qrope_ref[:, c * LANES:(c + 1) * LANES] = (qr_c * q_scale).astype(BF16)
    else:
        for h in range(heads):
            lo = hn + h * LANES
            qr_ref[h] = q[:, lo:lo + LANES] * cos + q[:, lo + heads * LANES:lo + (heads + 1) * LANES] * sin
            qn_h = q[:, h * NOPE_DIM:(h + 1) * NOPE_DIM].astype(BF16)
            ql_ref[h] = _dot(qn_h, wukt_ref[h])
    if prompt:
        ckv_b = c_kv.astype(BF16)
        kn_ref[...] = _dot(ckv_b, wuk_ref[...]).astype(BF16)
        vt_ref[...] = _dot_nt(wuvt_ref[...], ckv_b).astype(BF16)
        krb_ref[...] = k_r.astype(BF16)


def _mla_proj(x, g, wd_ext, qg, kvg, wuq_ext, cos, sin, extra, *, heads, seq_len, prompt, q_scale=None):
    t, d = x.shape
    tm = min(PROMPT_TILE, seq_len) if prompt else t
    tps = seq_len // tm if prompt else 1
    hn = heads * NOPE_DIM
    in_specs = [pl.BlockSpec((tm, d), lambda i: (i, 0)),
                _const_spec((1, d)), _const_spec(wd_ext.shape), _const_spec((1, Q_RANK)),
                _const_spec((1, KV_RANK)), _const_spec(wuq_ext.shape),
                pl.BlockSpec((tm, LANES), lambda i: (i % tps, 0)),
                pl.BlockSpec((tm, LANES), lambda i: (i % tps, 0))]
    in_specs += [_const_spec(w.shape) for w in extra]
    row = lambda w: pl.BlockSpec((tm, w), lambda i: (i, 0))
    if prompt:
        hr = heads * ROPE_DIM
        out_specs = [row(hn), row(hr), row(KV_RANK), row(LANES), row(hn),
                     pl.BlockSpec((hn, tm), lambda i: (0, i)), row(LANES)]
        out_shape = [jax.ShapeDtypeStruct((t, hn), BF16), jax.ShapeDtypeStruct((t, hr), BF16),
                     jax.ShapeDtypeStruct((t, KV_RANK), F32),
                     jax.ShapeDtypeStruct((t, LANES), F32), jax.ShapeDtypeStruct((t, hn), BF16),
                     jax.ShapeDtypeStruct((hn, t), BF16), jax.ShapeDtypeStruct((t, LANES), BF16)]
    else:
        head_major = lambda w: pl.BlockSpec((heads, tm, w), lambda i: (0, i, 0))
        out_specs = [head_major(KV_RANK), head_major(LANES), row(KV_RANK), row(LANES)]
        out_shape = [jax.ShapeDtypeStruct((heads, t, KV_RANK), F32), jax.ShapeDtypeStruct((heads, t, LANES), F32),
                     jax.ShapeDtypeStruct((t, KV_RANK), F32), jax.ShapeDtypeStruct((t, LANES), F32)]
    return pl.pallas_call(
        functools.partial(_mla_proj_kernel, heads=heads, prompt=prompt, q_scale=q_scale),
        grid=(t // tm,),
        in_specs=in_specs,
        out_specs=out_specs,
        out_shape=out_shape,
        compiler_params=_params("arbitrary"),
        name="mla_proj_prompt" if prompt else "mla_proj_sample",
    )(x, g.reshape(1, d), wd_ext, qg.reshape(1, Q_RANK), kvg.reshape(1, KV_RANK), wuq_ext, cos, sin, *extra)


def _attn_prompt_kernel(qn_ref, qr_ref, kn_ref, kr_ref, vt_ref, o_ref,
                        sa_scr, sb_scr, m_scr, acc_scr, *, t):
    i = pl.program_id(2)
    hps = sa_scr.shape[0]
    head = lambda g: slice(g * NOPE_DIM, (g + 1) * NOPE_DIM)
    qs = [jnp.concatenate([qn_ref[:, head(g)], qr_ref[:, g * ROPE_DIM:(g + 1) * ROPE_DIM]], axis=1)
          for g in range(hps)]
    m_scr[...] = jnp.full(m_scr.shape, -jnp.inf, F32)
    acc_scr[...] = jnp.zeros(acc_scr.shape, F32)
    ones = jnp.ones((BF16_SUBLANES, t), BF16)

    def scores(j, dst):
        ks = pl.multiple_of(j * t, t)
        kr = kr_ref[pl.ds(ks, t), :ROPE_DIM]
        for g in range(hps):
            k = jnp.concatenate([kn_ref[pl.ds(ks, t), head(g)], kr], axis=1)
            dst[g] = _dot_nt(k, qs[g])

    def update(src, j, diagonal):
        ks = pl.multiple_of(j * t, t)
        for g in range(hps):
            s = src[g]
            if diagonal:
                kpos = lax.broadcasted_iota(jnp.int32, (t, t), 0)
                qpos = lax.broadcasted_iota(jnp.int32, (t, t), 1)
                s = jnp.where(kpos <= qpos, s, -jnp.inf)
            m = m_scr[g]
            m_new = jnp.maximum(m, jnp.max(s, axis=0, keepdims=True))
            alpha = jnp.exp2(m - m_new)
            p = jnp.exp2(s - m_new).astype(BF16)
            v1 = jnp.concatenate([vt_ref[head(g), pl.ds(ks, t)], ones], axis=0)
            acc_scr[g] = alpha * acc_scr[g] + _dot(v1, p)
            m_scr[g] = m_new

    scores(0, sa_scr)

    def pair(jj, _):
        scores(2 * jj + 1, sb_scr)
        update(sa_scr, 2 * jj, False)
        scores(2 * jj + 2, sa_scr)
        update(sb_scr, 2 * jj + 1, False)
        return 0
    lax.fori_loop(0, i // 2, pair, 0)

    @pl.when(i % 2 == 0)
    def _():
        update(sa_scr, i, True)

    @pl.when(i % 2 == 1)
    def _():
        scores(i, sb_scr)
        update(sa_scr, i - 1, False)
        update(sb_scr, i, True)

    for g in range(hps):
        l = acc_scr[g, NOPE_DIM:NOPE_DIM + 1, :]
        o_ref[:, head(g)] = (acc_scr[g, :NOPE_DIM, :] * (1.0 / l)).T.astype(o_ref.dtype)


def _attn_prompt(qn, qrope, kn, krb, vt, *, heads, seq_len):
    tokens = qn.shape[0]
    bsz = tokens // seq_len
    t = min(ATTN_TILE, seq_len)
    nq = seq_len // t
    hps = ATTN_HEADS_PER_STEP
    hg = heads // hps
    w = hps * NOPE_DIM
    return pl.pallas_call(
        functools.partial(_attn_prompt_kernel, t=t),
        grid=(bsz, hg, nq),
        in_specs=[pl.BlockSpec((t, w), lambda b, h, i: (b * nq + i, h)),
                  pl.BlockSpec((t, hps * ROPE_DIM), lambda b, h, i: (b * nq + i, h)),
                  pl.BlockSpec((seq_len, w), lambda b, h, i: (b, h)),
                  pl.BlockSpec((seq_len, LANES), lambda b, h, i: (b, 0)),
                  pl.BlockSpec((w, seq_len), lambda b, h, i: (h, b))],
        out_specs=pl.BlockSpec((t, w), lambda b, h, i: (b * nq + i, h)),
        out_shape=jax.ShapeDtypeStruct((tokens, heads * NOPE_DIM), BF16),
        scratch_shapes=[pltpu.VMEM((hps, t, t), F32), pltpu.VMEM((hps, t, t), F32), pltpu.VMEM((hps, 1, t), F32),
                        pltpu.VMEM((hps, NOPE_DIM + BF16_SUBLANES, t), F32)],
        compiler_params=_params("arbitrary", "arbitrary", "arbitrary"),
        name="mla_attn_prompt",
    )(qn, qrope, kn, krb, vt)


def _attn_sample_kernel(pt_ref, ql_ref, qr_ref, cn_ref, kn_ref, lat_hbm, ropet_hbm, o_ref,
                        lat_buf, rope_buf, s_scr, sem, *, layer, n_pages, page, ppc, ts, scale):
    b = pl.program_id(0)
    nb = pl.num_programs(0)
    slot = b % 2
    heads = ql_ref.shape[0]
    spp = s_scr.shape[0]
    tk = ppc * page
    n_chunks = n_pages // ppc
    seq_rows = lambda k: slice(k * ts, (k + 1) * ts)

    def start_fetch(group, dst_slot):
        for k in range(spp):
            for p in range(n_pages):
                pg = pt_ref[group * spp + k, p]
                pltpu.make_async_copy(lat_hbm.at[layer, pg], lat_buf.at[dst_slot, k, p * page:(p + 1) * page, :],
                                      sem.at[0, dst_slot]).start()
                pltpu.make_async_copy(ropet_hbm.at[layer, pg], rope_buf.at[dst_slot, k, p],
                                      sem.at[1, dst_slot]).start()

    def wait_fetch(dst_slot):
        pltpu.make_async_copy(lat_buf.at[dst_slot], lat_buf.at[dst_slot], sem.at[0, dst_slot]).wait()
        pltpu.make_async_copy(rope_buf.at[dst_slot], rope_buf.at[dst_slot], sem.at[1, dst_slot]).wait()

    @pl.when(b == 0)
    def _():
        start_fetch(0, 0)

    start_fetch((b + 1) % nb, 1 - slot)
    wait_fetch(slot)

    qls = [ql_ref[:, seq_rows(k), :].reshape(heads * ts, KV_RANK).astype(BF16) for k in range(spp)]
    qrs = [qr_ref[:, seq_rows(k), :].reshape(heads * ts, LANES)[:, :ROPE_DIM].astype(BF16) for k in range(spp)]

    def lat_chunk(k, c):
        return lat_buf[slot, k, c * tk:(c + 1) * tk, :]

    for c in range(n_chunks):
        for k in range(spp):
            ropt = jnp.concatenate([rope_buf[slot, k, c * ppc + j] for j in range(ppc)], axis=1).astype(BF16)
            lat_t = lat_chunk(k, c).T.astype(BF16)
            s_scr[k, :, c * tk:(c + 1) * tk] = (_dot(qls[k], lat_t) + _dot(qrs[k], ropt)) * scale

    pad = BF16_SUBLANES - ts
    ms, ls, accs = [], [], []
    for k in range(spp):
        cn = jnp.concatenate([cn_ref[seq_rows(k), :], jnp.zeros((pad, KV_RANK), F32)], axis=0).astype(BF16)
        kn = jnp.concatenate([kn_ref[seq_rows(k), :][:, :ROPE_DIM], jnp.zeros((pad, ROPE_DIM), F32)],
                             axis=0).astype(BF16)
        s_new = (_dot_nt(qls[k], cn) + _dot_nt(qrs[k], kn)) * scale
        t_q = lax.broadcasted_iota(jnp.int32, s_new.shape, 0) % ts
        t_k = lax.broadcasted_iota(jnp.int32, s_new.shape, 1)
        s_new = jnp.where(t_k <= t_q, s_new, -jnp.inf)
        m = jnp.maximum(jnp.max(s_scr[k], axis=-1, keepdims=True), jnp.max(s_new, axis=-1, keepdims=True))
        p_new = jnp.exp(s_new - m)
        ms.append(m)
        ls.append(jnp.sum(p_new, axis=-1, keepdims=True))
        accs.append(_dot(p_new.astype(BF16), cn))
    for c in range(n_chunks):
        for k in range(spp):
            p = jnp.exp(s_scr[k, :, c * tk:(c + 1) * tk] - ms[k])
            ls[k] = ls[k] + jnp.sum(p, axis=-1, keepdims=True)
            accs[k] = accs[k] + _dot(p.astype(BF16), lat_chunk(k, c).astype(BF16))
    for k in range(spp):
        o_ref[:, seq_rows(k), :] = (accs[k] * (1.0 / ls[k])).reshape(heads, ts, KV_RANK)

    @pl.when(b == nb - 1)
    def _():
        wait_fetch(1 - slot)


def _attn_sample(page_table, ql, qr, ckv, kr, cache_lat, cache_rope_t, *, layer, ts, scale):
    heads, t, _ = ql.shape
    n_seq, n_pages = page_table.shape
    page = cache_lat.shape[2]
    n_past = n_pages * page
    ppc = min(PAST_CHUNK_PAGES, n_pages)
    spp = min(PAST_SEQS_PER_STEP, n_seq)
    rows = spp * ts
    grid_spec = pltpu.PrefetchScalarGridSpec(
        num_scalar_prefetch=1,
        grid=(n_seq // spp,),
        in_specs=[pl.BlockSpec((heads, rows, KV_RANK), lambda b, pt: (0, b, 0)),
                  pl.BlockSpec((heads, rows, LANES), lambda b, pt: (0, b, 0)),
                  pl.BlockSpec((rows, KV_RANK), lambda b, pt: (b, 0)),
                  pl.BlockSpec((rows, LANES), lambda b, pt: (b, 0)),
                  pl.BlockSpec(memory_space=pl.ANY),
                  pl.BlockSpec(memory_space=pl.ANY)],
        out_specs=pl.BlockSpec((heads, rows, KV_RANK), lambda b, pt: (0, b, 0)),
        scratch_shapes=[pltpu.VMEM((2, spp, n_past, KV_RANK), F32),
                        pltpu.VMEM((2, spp, n_pages, ROPE_DIM, page), F32),
                        pltpu.VMEM((spp, heads * ts, n_past), F32),
                        pltpu.SemaphoreType.DMA((2, 2))],
    )
    return pl.pallas_call(
        functools.partial(_attn_sample_kernel, layer=layer, n_pages=n_pages, page=page, ppc=ppc, ts=ts,
                          scale=scale),
        grid_spec=grid_spec,
        out_shape=jax.ShapeDtypeStruct((heads, t, KV_RANK), F32),
        compiler_params=_params("arbitrary"),
        name="mla_attn_sample",
    )(page_table, ql, qr, ckv, kr, cache_lat, cache_rope_t)


def _mla_sample_out_kernel(ol_ref, wuv_ref, wo_ref, x_ref, o_ref, os_scr):
    heads = ol_ref.shape[0]
    vd = wuv_ref.shape[2]
    for h in range(heads):
        os_scr[:, h * vd:(h + 1) * vd] = _dot(ol_ref[h].astype(BF16), wuv_ref[h]).astype(BF16)
    o_ref[...] = x_ref[...] + _dot(os_scr[...], wo_ref[...])


def _mla_sample_out(ol, wuv_h, wo, x):
    t, d = x.shape
    heads, _, vd = wuv_h.shape
    return pl.pallas_call(
        _mla_sample_out_kernel,
        grid=(1,),
        in_specs=[_const_spec(ol.shape), _const_spec(wuv_h.shape), _const_spec(wo.shape), _const_spec(x.shape)],
        out_specs=pl.BlockSpec((t, d), lambda i: (0, 0)),
        out_shape=jax.ShapeDtypeStruct((t, d), F32),
        scratch_shapes=[pltpu.VMEM((t, heads * vd), BF16)],
        compiler_params=_params("arbitrary"),
        name="mla_out_sample",
    )(ol, wuv_h, wo, x)


def _mem_kv_kernel(mem_ref, g_ref, wk_ref, wv_ref, k_ref, v_ref, mn_scr):
    @pl.when(pl.program_id(1) == 0)
    def _():
        mn_scr[...] = _rms(mem_ref[...], g_ref[...]).astype(BF16)
    mn = mn_scr[...]
    k_ref[...] = _dot(mn, wk_ref[...])
    v_ref[...] = _dot(mn, wv_ref[...])


def _mem_kv(mem, g, wkv, *, tn=512):
    m, d = mem.shape
    depth = wkv.shape[0]
    e = wkv.shape[2] // 2
    nk = e // tn
    return pl.pallas_call(
        _mem_kv_kernel,
        grid=(depth, nk),
        in_specs=[_const_spec((m, d)),
                  pl.BlockSpec((None, 1, d), lambda l, n: (l, 0, 0)),
                  pl.BlockSpec((None, d, tn), lambda l, n: (l, 0, n)),
                  pl.BlockSpec((None, d, tn), lambda l, n: (l, 0, nk + n))],
        out_specs=[pl.BlockSpec((None, m, tn), lambda l, n: (l, 0, n)),
                   pl.BlockSpec((None, m, tn), lambda l, n: (l, 0, n))],
        out_shape=[jax.ShapeDtypeStruct((depth, m, e), F32), jax.ShapeDtypeStruct((depth, m, e), F32)],
        scratch_shapes=[pltpu.VMEM((m, d), BF16)],
        compiler_params=_params("arbitrary", "arbitrary"),
        name="mem_kv",
    )(mem, g, wkv, wkv)


def _xattn_heads(q, k_of, v_of, heads, scale):
    e = q.shape[1] // heads
    outs = []
    for h in range(heads):
        s = _dot_nt(q[:, h * e:(h + 1) * e].astype(BF16), k_of(h)) * scale
        p = jnp.exp(s - jnp.max(s, axis=-1, keepdims=True))
        p = p * (1.0 / jnp.sum(p, axis=-1, keepdims=True))
        outs.append(_dot(p.astype(BF16), v_of(h)))
    return outs


def _xattn_prompt_kernel(*refs, heads, scale, has_mixer_out):
    if has_mixer_out:
        x_ref, a_ref, wa_ref, g_ref, wq_ref, mk_ref, mv_ref, wo_ref, o_ref, o_scr = refs
        x = x_ref[...] + _dot(a_ref[...], wa_ref[...])
    else:
        x_ref, g_ref, wq_ref, mk_ref, mv_ref, wo_ref, o_ref, o_scr = refs
        x = x_ref[...]
    q = _dot(_rms(x, g_ref[...]).astype(BF16), wq_ref[...])
    e = q.shape[1] // heads
    outs = _xattn_heads(q, lambda h: mk_ref[:, h * e:(h + 1) * e].astype(BF16),
                        lambda h: mv_ref[:, h * e:(h + 1) * e].astype(BF16), heads, scale)
    for h in range(heads):
        o_scr[:, h * e:(h + 1) * e] = outs[h].astype(BF16)
    o_ref[...] = x + _dot(o_scr[...], wo_ref[...])


def _xattn_prompt(x, g, wq, mk_all, mv_all, wo, *, layer, heads, seq_len, n_mem, scale, mixer_out=None):
    t, d = x.shape
    tm = min(XATTN_TILE, seq_len)
    tps = seq_len // tm
    row = lambda w: pl.BlockSpec((tm, w), lambda i: (i, 0))
    in_specs, args = [row(d)], [x]
    if mixer_out is not None:
        a, w_a = mixer_out
        in_specs += [row(a.shape[1]), _weight_spec(w_a)]
        args += [a, w_a]
    in_specs += [_weight_spec(g, layer), _weight_spec(wq, layer),
                 pl.BlockSpec((None, n_mem, d), lambda i: (layer, i // tps, 0)),
                 pl.BlockSpec((None, n_mem, d), lambda i: (layer, i // tps, 0)),
                 _weight_spec(wo, layer)]
    args += [g, wq, mk_all, mv_all, wo]
    return pl.pallas_call(
        functools.partial(_xattn_prompt_kernel, heads=heads, scale=scale, has_mixer_out=mixer_out is not None),
        grid=(t // tm,),
        in_specs=in_specs,
        out_specs=row(d),
        out_shape=jax.ShapeDtypeStruct((t, d), F32),
        scratch_shapes=[pltpu.VMEM((tm, d), BF16)],
        compiler_params=_params("arbitrary"),
        name="xattn_prompt",
    )(*args)


def _norm_matmul_kernel(x_ref, g_ref, w_ref, o_ref):
    o_ref[...] = _dot(_rms(x_ref[...], g_ref[...]).astype(BF16), w_ref[...])


def _norm_matmul(x, g, w, *, layer):
    t, d = x.shape
    n = w.shape[-1]
    return pl.pallas_call(
        _norm_matmul_kernel,
        grid=(1,),
        in_specs=[_const_spec((t, d)), _weight_spec(g, layer), _weight_spec(w, layer)],
        out_specs=pl.BlockSpec((t, n), lambda i: (0, 0)),
        out_shape=jax.ShapeDtypeStruct((t, n), F32),
        compiler_params=_params("arbitrary"),
        name="norm_matmul",
    )(x, g, w)


def _xattn_sample_kernel(q_ref, mk_ref, mv_ref, o_ref, *, ts, scale):
    nseq, _, heads, e = mk_ref.shape
    for s_i in range(nseq):
        q = q_ref[s_i * ts:(s_i + 1) * ts, :]
        q_hm = jnp.concatenate([q[:, h * e:(h + 1) * e] for h in range(heads)], axis=0).astype(BF16)
        k_all = mk_ref[s_i].reshape(-1, e).astype(BF16)
        v_all = mv_ref[s_i].reshape(-1, e).astype(BF16)
        s = _dot_nt(q_hm, k_all) * scale
        key_head = lax.broadcasted_iota(jnp.int32, s.shape, 1) % heads
        row_head = lax.broadcasted_iota(jnp.int32, s.shape, 0) // ts
        s = jnp.where(key_head == row_head, s, -jnp.inf)
        p = jnp.exp(s - jnp.max(s, axis=-1, keepdims=True))
        p = p * (1.0 / jnp.sum(p, axis=-1, keepdims=True))
        o = _dot(p.astype(BF16), v_all)
        for h in range(heads):
            o_ref[s_i * ts:(s_i + 1) * ts, h * e:(h + 1) * e] = o[h * ts:(h + 1) * ts]


def _xattn_sample(q, mk_all, mv_all, *, layer, ts, scale, seqs_per_step=8):
    t, d = q.shape
    _, n_seq, n_mem, heads, e = mk_all.shape
    sp = min(seqs_per_step, n_seq)
    cache_spec = pl.BlockSpec((None, sp, n_mem, heads, e), lambda i: (layer, i, 0, 0, 0))
    return pl.pallas_call(
        functools.partial(_xattn_sample_kernel, ts=ts, scale=scale),
        grid=(n_seq // sp,),
        in_specs=[pl.BlockSpec((sp * ts, d), lambda i: (i, 0)), cache_spec, cache_spec],
        out_specs=pl.BlockSpec((sp * ts, d), lambda i: (i, 0)),
        out_shape=jax.ShapeDtypeStruct((t, d), F32),
        compiler_params=_params("arbitrary"),
        name="xattn_sample",
    )(q, mk_all, mv_all)


def _rope_tables(pos):
    half = ROPE_DIM // 2
    inv = 1.0 / (ROPE_THETA ** (jnp.arange(half, dtype=F32) * (2.0 / ROPE_DIM)))
    ang = pos.astype(F32)[:, None] * inv[None, :]
    reps = LANES // half
    return jnp.tile(jnp.cos(ang), (1, reps)), jnp.tile(jnp.sin(ang), (1, reps))


def _rot_cols(w):
    half = w.shape[-1] // 2
    return jnp.concatenate([-w[..., half:], w[..., :half]], axis=-1)


def _pad_lanes(w):
    return jnp.pad(w, [(0, 0)] * (w.ndim - 1) + [(0, LANES - w.shape[-1])])


def kernel(x_prompt, x_sample, state_conv, cache_kv_latent, cache_k_rope, state_ffn_conv, cache_mem_k, cache_mem_v, page_table, mem_prompt, conv_norm_g, conv_w_in, conv_w, conv_w_out, mla_norm_g, mla_w_down, mla_q_norm_g, mla_kv_norm_g, mla_w_uq, mla_w_uk, mla_w_uv, mla_w_o, xa_norm_g, xa_mem_norm_g, xa_w_q, xa_w_kv, xa_w_o, ffn_norm_g, ffn_w_up, ffn_conv_w, ffn_w_down, final_norm_g):
    bp, sp, d = x_prompt.shape
    bs, ts, _ = x_sample.shape
    depth = ffn_w_up.shape[0]
    heads = mla_w_uq.shape[2]
    xa_heads, xa_dim = xa_w_q.shape[2], xa_w_q.shape[3]
    n_mem = mem_prompt.shape[1]
    n_pages, page = page_table.shape[1], cache_kv_latent.shape[2]
    score_scale = (NOPE_DIM + ROPE_DIM) ** -0.5
    xa_scale = xa_dim ** -0.5
    assert ts == SUBLANES and KV_RANK + Q_RANK + ROPE_DIM == mla_w_down.shape[2]

    xp = x_prompt.reshape(bp * sp, d)
    xs = x_sample.reshape(bs * ts, d)

    cos_p, sin_p = _rope_tables(jnp.arange(sp, dtype=jnp.int32))
    cos_s, sin_s = _rope_tables(n_pages * page + jnp.arange(ts, dtype=jnp.int32))
    cos_s, sin_s = jnp.tile(cos_s, (bs, 1)), jnp.tile(sin_s, (bs, 1))

    xa_e = xa_heads * xa_dim
    gain = lambda g: g.reshape(g.shape[0], 1, g.shape[1])
    conv_g, xa_g, ffn_g = gain(conv_norm_g), gain(xa_norm_g), gain(ffn_norm_g)
    w_in_all, w_out_all = conv_w_in.astype(BF16), conv_w_out.astype(BF16)
    wq_all = xa_w_q.reshape(depth, d, xa_e).astype(BF16)
    wo_x_all = xa_w_o.reshape(depth, xa_e, d).astype(BF16)
    w_up_all, w_dn_all = ffn_w_up.astype(BF16), ffn_w_down.astype(BF16)

    wkv_all = xa_w_kv.reshape(depth, d, 2 * xa_e).astype(BF16)
    mk_all, mv_all = _mem_kv(mem_prompt.reshape(bp * n_mem, d), gain(xa_mem_norm_g), wkv_all)
    cache_rope_t = jnp.swapaxes(cache_k_rope, 2, 3)

    conv_p, conv_s, lat_p, rop_p, lat_s, rop_s, ffn_p, ffn_s = [], [], [], [], [], [], [], []

    for i in range(depth):
        j = i // N_MIXERS
        prompt_mixer_out = None
        if i % N_MIXERS == 0:
            xp, st = _conv_mixer(xp, None, conv_g, w_in_all, conv_w, w_out_all, layer=j, seq_len=sp)
            conv_p.append(st[:, SUBLANES - 2:, :])
            xs, st = _conv_mixer(xs, state_conv, conv_g, w_in_all, conv_w, w_out_all, layer=j, seq_len=ts)
            conv_s.append(st)
        else:
            wd = mla_w_down[j]
            w_r = wd[:, Q_RANK + KV_RANK:]
            wd_ext = jnp.concatenate([wd[:, :Q_RANK + KV_RANK], _pad_lanes(w_r), _pad_lanes(_rot_cols(w_r))],
                                     axis=1).astype(BF16)
            wuq = mla_w_uq[j]
            wuq_r = wuq[:, :, NOPE_DIM:]
            wuq_n = wuq[:, :, :NOPE_DIM].reshape(Q_RANK, heads * NOPE_DIM)
            wuq_ext = jnp.concatenate([wuq_n, _pad_lanes(wuq_r).reshape(Q_RANK, heads * LANES),
                                       _pad_lanes(_rot_cols(wuq_r)).reshape(Q_RANK, heads * LANES)],
                                      axis=1).astype(BF16)
            wuq_packed = jnp.concatenate([wuq_n, wuq_r.reshape(Q_RANK, heads * ROPE_DIM),
                                          _rot_cols(wuq_r).reshape(Q_RANK, heads * ROPE_DIM)], axis=1).astype(BF16)
            wuk = mla_w_uk[j].astype(BF16)
            wuv = mla_w_uv[j].astype(BF16)
            wo = mla_w_o[j].reshape(heads * NOPE_DIM, d).astype(BF16)

            qn, qrope, ckv, kr, kn, vt, krb = _mla_proj(
                xp, mla_norm_g[j], wd_ext, mla_q_norm_g[j], mla_kv_norm_g[j], wuq_packed, cos_p, sin_p,
                [wuk.reshape(KV_RANK, heads * NOPE_DIM), wuv.reshape(KV_RANK, heads * NOPE_DIM).T],
                heads=heads, seq_len=sp, prompt=True, q_scale=score_scale * LOG2_E)
            op = _attn_prompt(qn, qrope, kn, krb, vt, heads=heads, seq_len=sp)
            prompt_mixer_out = (op, wo)
            lat_p.append(ckv.reshape(bp, sp, KV_RANK))
            rop_p.append(kr[:, :ROPE_DIM].reshape(bp, sp, ROPE_DIM))

            ql, qr, ckv_s, kr_s = _mla_proj(
                xs, mla_norm_g[j], wd_ext, mla_q_norm_g[j], mla_kv_norm_g[j], wuq_ext, cos_s, sin_s,
                [wuk.transpose(1, 2, 0)], heads=heads, seq_len=ts, prompt=False)
            ol = _attn_sample(page_table, ql, qr, ckv_s, kr_s, cache_kv_latent, cache_rope_t,
                              layer=j, ts=ts, scale=score_scale)
            xs = _mla_sample_out(ol, wuv.transpose(1, 0, 2), wo, xs)
            lat_s.append(ckv_s.reshape(bs, ts, KV_RANK))
            rop_s.append(kr_s[:, :ROPE_DIM].reshape(bs, ts, ROPE_DIM))

        xp = _xattn_prompt(xp, xa_g, wq_all, mk_all, mv_all, wo_x_all, layer=i, heads=xa_heads, seq_len=sp,
                           n_mem=n_mem, scale=xa_scale, mixer_out=prompt_mixer_out)
        qs = _norm_matmul(xs, xa_g, wq_all, layer=i)
        os_ = _xattn_sample(qs, cache_mem_k, cache_mem_v, layer=i, ts=ts, scale=xa_scale)

        fg = final_norm_g if i == depth - 1 else None
        xp, st = _conv_ffn(xp, None, ffn_g, w_up_all, ffn_conv_w, w_dn_all, fg, layer=i, seq_len=sp)
        ffn_p.append(st[:, SUBLANES - 2:, :])
        xs, st = _conv_ffn(xs, state_ffn_conv, ffn_g, w_up_all, ffn_conv_w, w_dn_all, fg, layer=i, seq_len=ts,
                           pending=(os_, wo_x_all))
        ffn_s.append(st)

    mem_shape = (depth, bp, n_mem, xa_heads, xa_dim)
    return (xp.reshape(bp, sp, d), xs.reshape(bs, ts, d),
            jnp.stack(conv_p), jnp.stack(conv_s),
            jnp.stack(lat_p), jnp.stack(rop_p), jnp.stack(lat_s), jnp.stack(rop_s),
            jnp.stack(ffn_p), jnp.stack(ffn_s),
            mk_all.reshape(mem_shape), mv_all.reshape(mem_shape))
```
